```python
import jax
import jax.numpy as jnp
from jax import lax
import numpy as np


D_MODEL = 1024
BATCH = 8
SEQ = 2048
DEPTH = 1

CHUNK = 64
HG_HEADS = 8
HG_DK = 128
HG_DV = 128
HG_KEY_WIDTH = HG_HEADS * HG_DK
HG_WIDTH = HG_HEADS * HG_DV
SB_HEADS = 16
SB_DH = 64
SB_WIDTH = SB_HEADS * SB_DH
SB_BLOCK = 128
FF_HIDDEN = (8 * D_MODEL + 3 * 256 - 1) // (3 * 256) * 256
IN_SPLITS = (HG_KEY_WIDTH, HG_KEY_WIDTH, HG_WIDTH, HG_WIDTH, SB_WIDTH, SB_WIDTH, SB_WIDTH, D_MODEL, D_MODEL)
IN_WIDTH = sum(IN_SPLITS)
EPS = 1e-6

kernel_name = 'hybrid_hgrn2_stickbreaking_block'


def rms_norm(x, gain):
    x32 = x.astype(jnp.float32)
    y = x32 * lax.rsqrt(jnp.mean(x32 * x32, axis=-1, keepdims=True) + EPS) * gain.astype(jnp.float32)
    return y.astype(x.dtype)


def split_cols(t, widths):
    outs, start = [], 0
    for w in widths:
        outs.append(t[..., start:start + w])
        start += w
    return outs


def hgrn2_mixer(q, f_raw, i, lb):
    b, s, _ = q.shape
    n = s // CHUNK
    lb = lb.astype(jnp.float32)
    f = lb + (1.0 - lb) * jax.nn.sigmoid(f_raw.astype(jnp.float32))
    log_f = jnp.log(f)
    k = 1.0 - f

    def chunks(t, d):
        return t.astype(jnp.float32).reshape(b, n, CHUNK, HG_HEADS, d).transpose(1, 0, 3, 2, 4)

    qc, kc, gc = chunks(q, HG_DK), chunks(k, HG_DK), chunks(log_f, HG_DK)
    vc = chunks(i, HG_DV)
    causal = jnp.tril(jnp.ones((CHUNK, CHUNK), dtype=bool))[:, :, None]

    def step(state, inp):
        q_c, k_c, v_c, g_c = inp
        cum = jnp.cumsum(g_c, axis=-2)
        diff = cum[:, :, :, None, :] - cum[:, :, None, :, :]
        decay = jnp.exp(jnp.where(causal, diff, -jnp.inf))
        scores = jnp.einsum('bhtd,bhtsd,bhsd->bhts', q_c, decay, k_c)
        o = (jnp.einsum('bhts,bhsv->bhtv', scores, v_c)
             + jnp.einsum('bhtd,bhdv->bhtv', q_c * jnp.exp(cum), state))
        last = cum[:, :, -1:, :]
        state = (jnp.exp(last[:, :, 0, :, None]) * state
                 + jnp.einsum('bhsd,bhsv->bhdv', k_c * jnp.exp(last - cum), v_c))
        return state, o

    s0 = jnp.zeros((b, HG_HEADS, HG_DK, HG_DV), jnp.float32)
    _, o = lax.scan(step, s0, (qc, kc, vc, gc))
    return o.transpose(1, 0, 3, 2, 4).reshape(b, s, HG_HEADS, HG_DV)


def stick_breaking(q, k, v):
    s_len = q.shape[2]
    scale = 1.0 / float(np.sqrt(SB_DH))
    outs = []
    for blk in range(s_len // SB_BLOCK):
        q0 = blk * SB_BLOCK
        end = q0 + SB_BLOCK
        z = jnp.einsum('bhqd,bhkd->bhqk', q[:, :, q0:end], k[:, :, :end]) * scale
        qpos = q0 + jnp.arange(SB_BLOCK)
        kpos = jnp.arange(end)
        mask = kpos[None, :] < qpos[:, None]
        log_keep = jnp.where(mask, jax.nn.log_sigmoid(-z), 0.0)
        log_w = jax.nn.log_sigmoid(z) + lax.cumsum(log_keep, axis=3, reverse=True) - log_keep
        a = jnp.where(mask, jnp.exp(log_w), 0.0)
        outs.append(jnp.einsum('bhqk,bhkd->bhqd', a, v[:, :, :end]))
    return jnp.concatenate(outs, axis=2)


def setup_inputs(seed: int = 0) -> dict:
    key = jax.random.key(seed)
    ks = jax.random.split(key, 14)
    f32 = jnp.float32
    nrm = lambda k, shape, fan: jax.random.normal(k, shape, f32) * (fan ** -0.5)
    gain = lambda k, shape: 1.0 + 0.02 * jax.random.normal(k, shape, f32)
    return {
        'x': jax.random.normal(ks[0], (BATCH, SEQ, D_MODEL), f32),
        'norm1_gain': gain(ks[1], (DEPTH, D_MODEL)),
        'w_in': nrm(ks[2], (DEPTH, D_MODEL, IN_WIDTH), D_MODEL),
        'lb_logits': 0.5 * jax.random.normal(ks[3], (DEPTH + 1, HG_KEY_WIDTH), f32),
        'hg_out_norm': gain(ks[4], (DEPTH, HG_HEADS, HG_DV)),
        'sb_q_norm': gain(ks[5], (DEPTH, SB_HEADS, SB_DH)),
        'sb_k_norm': gain(ks[6], (DEPTH, SB_HEADS, SB_DH)),
        'w_hg_out': nrm(ks[7], (DEPTH, HG_WIDTH, D_MODEL), HG_WIDTH),
        'w_sb_out': nrm(ks[8], (DEPTH, SB_WIDTH, D_MODEL), SB_WIDTH),
        'w_o': nrm(ks[9], (DEPTH, D_MODEL, D_MODEL), D_MODEL),
        'norm2_gain': gain(ks[10], (DEPTH, D_MODEL)),
        'w_ffn_in': nrm(ks[11], (DEPTH, D_MODEL, 2 * FF_HIDDEN), D_MODEL),
        'w_ffn_out': nrm(ks[12], (DEPTH, FF_HIDDEN, D_MODEL), FF_HIDDEN),
    }


def reference(x, norm1_gain, w_in, lb_logits, hg_out_norm, sb_q_norm, sb_k_norm,
              w_hg_out, w_sb_out, w_o, norm2_gain, w_ffn_in, w_ffn_out):
    b, s, _ = x.shape
    lower_bounds = jnp.cumsum(jax.nn.softmax(lb_logits.astype(jnp.float32), axis=0), axis=0)
    for layer in range(DEPTH):
        h = rms_norm(x, norm1_gain[layer])
        proj = h @ w_in[layer]
        hq, hf, hi, hg, sq, sk, sv, ga, gb = split_cols(proj, IN_SPLITS)

        o_hg = hgrn2_mixer(hq, hf, hi, lower_bounds[layer])
        o_hg = rms_norm(o_hg, hg_out_norm[layer]).reshape(b, s, HG_WIDTH)
        o_hg = o_hg * jax.nn.sigmoid(hg.astype(jnp.float32))
        y_hg = o_hg.astype(x.dtype) @ w_hg_out[layer]

        def heads(t):
            return t.astype(jnp.float32).reshape(b, s, SB_HEADS, SB_DH)
        q = rms_norm(heads(sq), sb_q_norm[layer]).transpose(0, 2, 1, 3)
        k = rms_norm(heads(sk), sb_k_norm[layer]).transpose(0, 2, 1, 3)
        v = heads(sv).transpose(0, 2, 1, 3)
        o_sb = stick_breaking(q, k, v).transpose(0, 2, 1, 3).reshape(b, s, SB_WIDTH)
        y_sb = o_sb.astype(x.dtype) @ w_sb_out[layer]

        mixed = jax.nn.sigmoid(ga) * y_hg + jax.nn.sigmoid(gb) * y_sb
        x = x + mixed @ w_o[layer]

        h2 = rms_norm(x, norm2_gain[layer])
        gate, up = split_cols(h2 @ w_ffn_in[layer], (FF_HIDDEN, FF_HIDDEN))
        x = x + (jax.nn.silu(gate) * up) @ w_ffn_out[layer]
    return x
```

```python
import functools

import jax
import jax.numpy as jnp
from jax import lax
from jax.experimental import pallas as pl
from jax.experimental.pallas import tpu as pltpu

D_MODEL = 1024
HG_HEADS = 8
HG_DK = 128
HG_DV = 128
SB_HEADS = 16
SB_DH = 64
FF_HIDDEN = 2816
N_GROUPS = 9
EPS = 1e-6

LANES = 128
HG_CHUNK = 64
HG_EXP_CLAMP = 80.0
SB_TQ = 128
SB_TK = 128
VMEM_LIMIT = 56 * 1024 * 1024

f32 = jnp.float32
bf16 = jnp.bfloat16


def _sigmoid(x):
    return 1.0 / (1.0 + jnp.exp(-x))


def _softplus(z):
    return jnp.maximum(z, 0.0) + jnp.log(1.0 + jnp.exp(-jnp.abs(z)))


def _dot(a, b):
    return jnp.dot(a, b, preferred_element_type=f32)


def _dot_nt(a, b):
    return lax.dot_general(a, b, (((1,), (1,)), ((), ())), preferred_element_type=f32)


def _dot_tn(a, b):
    return lax.dot_general(a, b, (((0,), (0,)), ((), ())), preferred_element_type=f32)


def _inproj_kernel(x_ref, gain_ref, w_ref, o_ref, h_scr):
    @pl.when(pl.program_id(1) == 0)
    def _():
        x = x_ref[...]
        ms = jnp.mean(x * x, axis=-1, keepdims=True)
        h_scr[...] = (x * lax.rsqrt(ms + EPS) * gain_ref[...]).astype(bf16)

    o_ref[...] = _dot(h_scr[...], w_ref[...])


def _in_proj(x2, gain, w_bf, tm=1024, tn=1024):
    t = x2.shape[0]
    n = w_bf.shape[1]
    return pl.pallas_call(
        _inproj_kernel,
        grid=(t // tm, n // tn),
        in_specs=[
            pl.BlockSpec((tm, D_MODEL), lambda i, j: (i, 0)),
            pl.BlockSpec((1, D_MODEL), lambda i, j: (0, 0)),
            pl.BlockSpec((D_MODEL, tn), lambda i, j: (0, j)),
        ],
        out_specs=pl.BlockSpec((tm, tn), lambda i, j: (i, j)),
        out_shape=jax.ShapeDtypeStruct((t, n), f32),
        scratch_shapes=[pltpu.VMEM((tm, D_MODEL), bf16)],
        compiler_params=pltpu.CompilerParams(
            dimension_semantics=("arbitrary", "arbitrary"), vmem_limit_bytes=VMEM_LIMIT),
        name="in_proj",
    )(x2, gain, w_bf)


def _split3(g):
    g1 = g.astype(bf16)
    r1 = g - g1.astype(f32)
    g2 = r1.astype(bf16)
    g3 = (r1 - g2.astype(f32)).astype(bf16)
    return g1, g2, g3


def _hgrn_kernel(q_ref, f_ref, i_ref, og_ref, lbl_ref, gain_ref, o_ref, st_scr, *, layer):
    c_len = HG_CHUNK
    n_chunks = q_ref.shape[0] // c_len

    l = lbl_ref[...]
    e = jnp.exp(l - jnp.max(l, axis=0, keepdims=True))
    lb = jnp.sum(e[0:layer + 1, :], axis=0, keepdims=True) / jnp.sum(e, axis=0, keepdims=True)
    gain = gain_ref[...]

    row = lax.broadcasted_iota(jnp.int32, (c_len, c_len), 0)
    col = lax.broadcasted_iota(jnp.int32, (c_len, c_len), 1)
    causal = row >= col
    tri = jnp.where(causal, 1.0, 0.0).astype(bf16)

    st_scr[...] = jnp.zeros_like(st_scr)

    def body(c, carry):
        rows = pl.ds(pl.multiple_of(c * c_len, c_len), c_len)
        q = q_ref[rows, :]
        v = i_ref[rows, :]
        f = lb + (1.0 - lb) * _sigmoid(f_ref[rows, :])
        g = jnp.log(f)
        k = 1.0 - f
        g1, g2, g3 = _split3(g)
        cum = _dot(tri, g1) + _dot(tri, g2) + _dot(tri, g3)
        last = cum[c_len - 1:c_len, :]
        mid = cum[c_len // 2 - 1:c_len // 2, :]
        v_bf = v.astype(bf16)

        qt = (q * jnp.exp(jnp.minimum(cum - mid, HG_EXP_CLAMP))).astype(bf16)
        kt = (k * jnp.exp(jnp.minimum(mid - cum, HG_EXP_CLAMP))).astype(bf16)
        scores = jnp.where(causal, _dot_nt(qt, kt), 0.0)
        o = _dot(scores.astype(bf16), v_bf)

        st = st_scr[...]
        qd = (q * jnp.exp(cum)).astype(bf16)
        o = o + _dot_nt(qd, st.astype(bf16))
        kd = (k * jnp.exp(last - cum)).astype(bf16)
        st_scr[...] = st * jnp.exp(last) + _dot_tn(v_bf, kd)

        ms = jnp.mean(o * o, axis=-1, keepdims=True)
        o = o * lax.rsqrt(ms + EPS) * gain * _sigmoid(og_ref[rows, :])
        o_ref[rows, :] = o.astype(o_ref.dtype)
        return carry

    lax.fori_loop(0, n_chunks, body, 0)


def _hgrn2(proj, lb_logits, hg_gain, layer, batch, seq):
    t = proj.shape[0]
    hpg = D_MODEL // LANES

    def col(group):
        return pl.BlockSpec((seq, LANES), lambda b, h: (b, group * hpg + h))

    return pl.pallas_call(
        functools.partial(_hgrn_kernel, layer=layer),
        grid=(batch, HG_HEADS),
        in_specs=[
            col(0), col(1), col(2), col(3),
            pl.BlockSpec((lb_logits.shape[0], HG_DK), lambda b, h: (0, h)),
            pl.BlockSpec((None, 1, HG_DV), lambda b, h: (h, 0, 0)),
        ],
        out_specs=pl.BlockSpec((seq, HG_DV), lambda b, h: (b, h)),
        out_shape=jax.ShapeDtypeStruct((t, HG_HEADS * HG_DV), bf16),
        scratch_shapes=[pltpu.VMEM((HG_DV, HG_DK), f32)],
        compiler_params=pltpu.CompilerParams(
            dimension_semantics=("arbitrary", "arbitrary"), vmem_limit_bytes=VMEM_LIMIT),
        name="hgrn2",
    )(proj, proj, proj, proj, lb_logits, hg_gain.reshape(HG_HEADS, 1, HG_DV))


def _pair_rms(x, gain, lo_mask):
    x2 = x * x
    s_lo = jnp.sum(jnp.where(lo_mask, x2, 0.0), axis=-1, keepdims=True)
    s_hi = jnp.sum(jnp.where(lo_mask, 0.0, x2), axis=-1, keepdims=True)
    ms = jnp.where(lo_mask, s_lo, s_hi) * (1.0 / SB_DH)
    return x * lax.rsqrt(ms + EPS) * gain


def _split2(x):
    hi = x.astype(bf16)
    lo = (x - hi.astype(f32)).astype(bf16)
    return hi, lo


def _sb_kernel(q_ref, k_ref, v_ref, qg_ref, kg_ref, o_ref, kn_scr, vb_scr):
    qi = pl.program_id(2)
    seq = k_ref.shape[0]
    lo_mask = lax.broadcasted_iota(jnp.int32, (1, LANES), 1) < SB_DH

    @pl.when(qi == 0)
    def _():
        prep = 256

        def prep_body(r, carry):
            rows = pl.ds(pl.multiple_of(r * prep, prep), prep)
            kn = _pair_rms(k_ref[rows, :], kg_ref[...], lo_mask).astype(bf16)
            vb = v_ref[rows, :].astype(bf16)
            for h in range(2):
                kn_scr[h, rows, :] = kn[:, h * SB_DH:(h + 1) * SB_DH]
                vb_scr[h, rows, :] = vb[:, h * SB_DH:(h + 1) * SB_DH]
            return carry

        lax.fori_loop(0, seq // prep, prep_body, 0)

    scale = 1.0 / (SB_DH ** 0.5)
    qn = (_pair_rms(q_ref[...], qg_ref[...], lo_mask) * scale).astype(bf16)

    row = lax.broadcasted_iota(jnp.int32, (SB_TQ, SB_TK), 0)
    col = lax.broadcasted_iota(jnp.int32, (SB_TQ, SB_TK), 1)
    earlier = col < row
    urow = lax.broadcasted_iota(jnp.int32, (SB_TK, SB_TK), 0)
    ucol = lax.broadcasted_iota(jnp.int32, (SB_TK, SB_TK), 1)
    suffix = jnp.where(urow >= ucol, 1.0, 0.0).astype(bf16)

    def tile(h, qh, kb, carry, diag):
        keys = pl.ds(pl.multiple_of(kb * SB_TK, SB_TK), SB_TK)
        z = _dot_nt(qh, kn_scr[h, keys, :])
        sp = _softplus(z)
        if diag:
            sp = jnp.where(earlier, sp, 0.0)
        hi, lo = _split2(sp)
        r = _dot(hi, suffix) + _dot(lo, suffix) + carry
        a = jnp.exp(z - r)
        if diag:
            a = jnp.where(earlier, a, 0.0)
        return _dot(a.astype(bf16), vb_scr[h, keys, :]), r[:, 0:1]

    for h in range(2):
        qh = qn[:, h * SB_DH:(h + 1) * SB_DH]
        acc, carry = tile(h, qh, qi, jnp.zeros((SB_TQ, 1), f32), True)

        def body(j, state, h=h, qh=qh):
            acc, carry = state
            pv, carry = tile(h, qh, qi - 1 - j, carry, False)
            return acc + pv, carry

        acc, _ = lax.fori_loop(0, qi, body, (acc, carry))
        o_ref[:, h * SB_DH:(h + 1) * SB_DH] = acc.astype(o_ref.dtype)


def _sb_attn(proj, q_gain, k_gain, batch, seq):
    t = proj.shape[0]
    hpg = D_MODEL // LANES
    n_pairs = SB_HEADS // 2
    n_qt = seq // SB_TQ
    return pl.pallas_call(
        _sb_kernel,
        grid=(batch, n_pairs, n_qt),
        in_specs=[
            pl.BlockSpec((SB_TQ, LANES), lambda b, p, i: (b * n_qt + i, 4 * hpg + p)),
            pl.BlockSpec((seq, LANES), lambda b, p, i: (b, 5 * hpg + p)),
            pl.BlockSpec((seq, LANES), lambda b, p, i: (b, 6 * hpg + p)),
            pl.BlockSpec((None, 1, LANES), lambda b, p, i: (p, 0, 0)),
            pl.BlockSpec((None, 1, LANES), lambda b, p, i: (p, 0, 0)),
        ],
        out_specs=pl.BlockSpec((SB_TQ, LANES), lambda b, p, i: (b * n_qt + i, p)),
        out_shape=jax.ShapeDtypeStruct((t, SB_HEADS * SB_DH), bf16),
        scratch_shapes=[pltpu.VMEM((2, seq, SB_DH), bf16), pltpu.VMEM((2, seq, SB_DH), bf16)],
        compiler_params=pltpu.CompilerParams(
            dimension_semantics=("arbitrary", "arbitrary", "arbitrary"), vmem_limit_bytes=VMEM_LIMIT),
        name="sb_attn",
    )(proj, proj, proj, q_gain.reshape(n_pairs, 1, LANES), k_gain.reshape(n_pairs, 1, LANES))


def _merge_kernel(x_ref, ohg_ref, osb_ref, ga_ref, gb_ref, whg_ref, wsb_ref, wo_ref, o_ref):
    y_hg = _dot(ohg_ref[...], whg_ref[...])
    y_sb = _dot(osb_ref[...], wsb_ref[...])
    mixed = _sigmoid(ga_ref[...]) * y_hg + _sigmoid(gb_ref[...]) * y_sb
    o_ref[...] = x_ref[...] + _dot(mixed.astype(bf16), wo_ref[...])


def _merge(x2, o_hg, o_sb, proj, w_hg, w_sb, w_o, tm=512):
    t = x2.shape[0]
    rows = lambda i: (i, 0)
    whole = lambda i: (0, 0)
    return pl.pallas_call(
        _merge_kernel,
        grid=(t // tm,),
        in_specs=[
            pl.BlockSpec((tm, D_MODEL), rows),
            pl.BlockSpec((tm, D_MODEL), rows),
            pl.BlockSpec((tm, D_MODEL), rows),
            pl.BlockSpec((tm, D_MODEL), lambda i: (i, 7)),
            pl.BlockSpec((tm, D_MODEL), lambda i: (i, 8)),
            pl.BlockSpec((D_MODEL, D_MODEL), whole),
            pl.BlockSpec((D_MODEL, D_MODEL), whole),
            pl.BlockSpec((D_MODEL, D_MODEL), whole),
        ],
        out_specs=pl.BlockSpec((tm, D_MODEL), rows),
        out_shape=jax.ShapeDtypeStruct((t, D_MODEL), f32),
        compiler_params=pltpu.CompilerParams(
            dimension_semantics=("arbitrary",), vmem_limit_bytes=VMEM_LIMIT),
        name="merge",
    )(x2, o_hg, o_sb, proj, proj, w_hg, w_sb, w_o)


def _ffn_kernel(x_ref, gain_ref, wg_ref, wu_ref, wd_ref, o_ref, h_scr):
    @pl.when(pl.program_id(1) == 0)
    def _():
        x = x_ref[...]
        ms = jnp.mean(x * x, axis=-1, keepdims=True)
        h_scr[...] = (x * lax.rsqrt(ms + EPS) * gain_ref[...]).astype(bf16)
        o_ref[...] = x

    h = h_scr[...]
    gate = _dot(h, wg_ref[...])
    up = _dot(h, wu_ref[...])
    act = (gate * _sigmoid(gate) * up).astype(bf16)
    o_ref[...] += _dot(act, wd_ref[...])


def _ffn(x1, gain, w_in_bf, w_out_bf, tm=1024, th=256):
    t = x1.shape[0]
    n_h = FF_HIDDEN // th
    return pl.pallas_call(
        _ffn_kernel,
        grid=(t // tm, n_h),
        in_specs=[
            pl.BlockSpec((tm, D_MODEL), lambda i, j: (i, 0)),
            pl.BlockSpec((1, D_MODEL), lambda i, j: (0, 0)),
            pl.BlockSpec((D_MODEL, th), lambda i, j: (0, j)),
            pl.BlockSpec((D_MODEL, th), lambda i, j: (0, n_h + j)),
            pl.BlockSpec((th, D_MODEL), lambda i, j: (j, 0)),
        ],
        out_specs=pl.BlockSpec((tm, D_MODEL), lambda i, j: (i, 0)),
        out_shape=jax.ShapeDtypeStruct((t, D_MODEL), f32),
        scratch_shapes=[pltpu.VMEM((tm, D_MODEL), bf16)],
        compiler_params=pltpu.CompilerParams(
            dimension_semantics=("arbitrary", "arbitrary"), vmem_limit_bytes=VMEM_LIMIT),
        name="ffn",
    )(x1, gain, w_in_bf, w_in_bf, w_out_bf)


def kernel(x, norm1_gain, w_in, lb_logits, hg_out_norm, sb_q_norm, sb_k_norm,
           w_hg_out, w_sb_out, w_o, norm2_gain, w_ffn_in, w_ffn_out):
    batch, seq, d = x.shape
    depth = norm1_gain.shape[0]
    x2 = x.reshape(batch * seq, d)
    for layer in range(depth):
        proj = _in_proj(x2, norm1_gain[layer][None, :], w_in[layer].astype(bf16))
        o_hg = _hgrn2(proj, lb_logits, hg_out_norm[layer], layer, batch, seq)
        o_sb = _sb_attn(proj, sb_q_norm[layer], sb_k_norm[layer], batch, seq)
        x1 = _merge(x2, o_hg, o_sb, proj, w_hg_out[layer].astype(bf16),
                    w_sb_out[layer].astype(bf16), w_o[layer].astype(bf16))
        x2 = _ffn(x1, norm2_gain[layer][None, :], w_ffn_in[layer].astype(bf16),
                  w_ffn_out[layer].astype(bf16))
    return x2.reshape(batch, seq, d)
```

```python
import functools
import math

import jax
import jax.numpy as jnp
from jax import lax
from jax.experimental import pallas as pl
from jax.experimental.pallas import tpu as pltpu

D_MODEL = 1024
HG_HEADS = 8
HG_DK = 128
HG_DV = 128
SB_HEADS = 16
SB_DH = 64
FF_HIDDEN = 2816
EPS = 1e-6
G_HQ, G_HF, G_HI, G_HG, G_SQ, G_SK, G_SV, G_GA, G_GB = range(9)

LANES = 128
HG_CHUNK = 64
HG_ROWS = 512
HG_EXP_CLAMP = 80.0
SB_T = 128
SB_QROWS = 512
SB_DONE_LOG2 = 128.0
VMEM_LIMIT = 56 * 1024 * 1024

f32 = jnp.float32
bf16 = jnp.bfloat16
NEG_INF = float("-inf")


def _sigmoid(x):
    return 1.0 / (1.0 + jnp.exp(-x))


def _dot(a, b):
    return jnp.dot(a, b, preferred_element_type=f32)


def _dot_nt(a, b):
    return lax.dot_general(a, b, (((1,), (1,)), ((), ())), preferred_element_type=f32)


def _dot_tn(a, b):
    return lax.dot_general(a, b, (((0,), (0,)), ((), ())), preferred_element_type=f32)


def _split2(x):
    hi = x.astype(bf16)
    lo = (x - hi.astype(f32)).astype(bf16)
    return hi, lo


def _split3(g):
    g1 = g.astype(bf16)
    r1 = g - g1.astype(f32)
    g2 = r1.astype(bf16)
    g3 = (r1 - g2.astype(f32)).astype(bf16)
    return g1, g2, g3


def _inproj_kernel(x_ref, gain_ref, w_ref, o_ref, f_ref, h_scr):
    j = pl.program_id(1)

    @pl.when(j == 0)
    def _():
        x = x_ref[...]
        ms = jnp.mean(x * x, axis=-1, keepdims=True)
        h_scr[...] = (x * lax.rsqrt(ms + EPS) * gain_ref[...]).astype(bf16)

    acc = _dot(h_scr[...], w_ref[...])
    o_ref[...] = acc.astype(o_ref.dtype)

    @pl.when(j == G_HF)
    def _():
        f_ref[...] = acc


def _in_proj(x2, gain, w_bf, tm=1024):
    t = x2.shape[0]
    n = w_bf.shape[1]
    return pl.pallas_call(
        _inproj_kernel,
        grid=(t // tm, n // D_MODEL),
        in_specs=[
            pl.BlockSpec((tm, D_MODEL), lambda i, j: (i, 0)),
            pl.BlockSpec((1, D_MODEL), lambda i, j: (0, 0)),
            pl.BlockSpec((D_MODEL, D_MODEL), lambda i, j: (0, j)),
        ],
        out_specs=[
            pl.BlockSpec((tm, D_MODEL), lambda i, j: (i, j)),
            pl.BlockSpec((tm, D_MODEL), lambda i, j: (i, 0)),
        ],
        out_shape=[
            jax.ShapeDtypeStruct((t, n), bf16),
            jax.ShapeDtypeStruct((t, D_MODEL), f32),
        ],
        scratch_shapes=[pltpu.VMEM((tm, D_MODEL), bf16)],
        compiler_params=pltpu.CompilerParams(
            dimension_semantics=("arbitrary", "arbitrary"), vmem_limit_bytes=VMEM_LIMIT),
        name="in_proj",
    )(x2, gain, w_bf)


def _hgrn_kernel(q_ref, f_ref, i_ref, og_ref, lbl_ref, gain_ref, o_ref, st_scr, *, layer):
    c_len = HG_CHUNK
    n_chunks = q_ref.shape[0] // c_len

    @pl.when(pl.program_id(1) == 0)
    def _():
        st_scr[...] = jnp.zeros_like(st_scr)

    l = lbl_ref[...]
    e = jnp.exp(l - jnp.max(l, axis=0, keepdims=True))
    lb = jnp.sum(e[0:layer + 1, :], axis=0, keepdims=True) / jnp.sum(e, axis=0, keepdims=True)
    gain = gain_ref[...]

    row = lax.broadcasted_iota(jnp.int32, (c_len, c_len), 0)
    col = lax.broadcasted_iota(jnp.int32, (c_len, c_len), 1)
    causal = row >= col
    tri = jnp.where(causal, 1.0, 0.0).astype(bf16)
    tri3 = jnp.concatenate([tri, tri, tri], axis=1)

    def body(c, carry):
        rows = pl.ds(pl.multiple_of(c * c_len, c_len), c_len)
        f = lb + (1.0 - lb) * _sigmoid(f_ref[rows, :])
        g = jnp.log(f)
        k = 1.0 - f
        cum = _dot(tri3, jnp.concatenate(_split3(g), axis=0))
        last = cum[c_len - 1:c_len, :]
        mid = cum[c_len // 2 - 1:c_len // 2, :]
        q = q_ref[rows, :].astype(f32)
        qt = (q * jnp.exp(jnp.minimum(cum - mid, HG_EXP_CLAMP))).astype(bf16)
        kt = (k * jnp.exp(jnp.minimum(mid - cum, HG_EXP_CLAMP))).astype(bf16)
        qd = (q * jnp.exp(cum)).astype(bf16)
        kd = (k * jnp.exp(last - cum)).astype(bf16)
        e_last = jnp.exp(last)
        v = i_ref[rows, :]
        out_scale = gain * _sigmoid(og_ref[rows, :].astype(f32))

        for h in range(HG_HEADS):
            sl = slice(h * HG_DK, (h + 1) * HG_DK)
            scores = jnp.where(causal, _dot_nt(qt[:, sl], kt[:, sl]), 0.0)
            st = st_scr[h]
            o = _dot(scores.astype(bf16), v[:, sl]) + _dot_nt(qd[:, sl], st.astype(bf16))
            st_scr[h] = st * e_last[:, sl] + _dot_tn(v[:, sl], kd[:, sl])
            ms = jnp.mean(o * o, axis=-1, keepdims=True)
            o_ref[rows, sl] = (o * lax.rsqrt(ms + EPS) * out_scale[:, sl]).astype(o_ref.dtype)
        return carry

    lax.fori_loop(0, n_chunks, body, 0)


def _hgrn2(proj, hf, lb_logits, hg_gain, layer, batch, seq):
    t = proj.shape[0]
    n_rt = seq // HG_ROWS

    def grp(group):
        return pl.BlockSpec((HG_ROWS, D_MODEL), lambda b, r: (b * n_rt + r, group))

    return pl.pallas_call(
        functools.partial(_hgrn_kernel, layer=layer),
        grid=(batch, n_rt),
        in_specs=[
            grp(G_HQ), grp(0), grp(G_HI), grp(G_HG),
            pl.BlockSpec(lb_logits.shape, lambda b, r: (0, 0)),
            pl.BlockSpec((1, D_MODEL), lambda b, r: (0, 0)),
        ],
        out_specs=grp(0),
        out_shape=jax.ShapeDtypeStruct((t, HG_HEADS * HG_DV), bf16),
        scratch_shapes=[pltpu.VMEM((HG_HEADS, HG_DV, HG_DK), f32)],
        compiler_params=pltpu.CompilerParams(
            dimension_semantics=("arbitrary", "arbitrary"), vmem_limit_bytes=VMEM_LIMIT),
        name="hgrn2",
    )(proj, hf, proj, proj, lb_logits, hg_gain.reshape(1, HG_HEADS * HG_DV))


def _pair_rms(x, gain, group_mean):
    hi, lo = _split2(x * x)
    ms = _dot(hi, group_mean) + _dot(lo, group_mean)
    return x * lax.rsqrt(ms + EPS) * gain


def _softplus2(z):
    return jnp.maximum(z, 0.0) + jnp.log2(1.0 + jnp.exp2(-jnp.abs(z)))


def _suffix_sums(sp, suffix2):
    full = _dot(jnp.concatenate(_split2(sp), axis=1), suffix2)
    return full[:, :SB_T], full[:, SB_T:]


def _sb_kernel(q_ref, k_ref, v_ref, qg_ref, kg_ref, o_ref, kn_scr, acc_scr, car_scr):
    step = pl.program_id(2)
    seq = k_ref.shape[0]
    n_sub = SB_QROWS // SB_T
    lane = lax.broadcasted_iota(jnp.int32, (1, LANES), 1)
    lo_mask = lane < SB_DH
    head_mask = (lo_mask, jnp.logical_not(lo_mask))

    gr = lax.broadcasted_iota(jnp.int32, (LANES, LANES), 0) // SB_DH
    gc = lax.broadcasted_iota(jnp.int32, (LANES, LANES), 1) // SB_DH
    group_mean = jnp.where(gr == gc, 1.0 / SB_DH, 0.0).astype(bf16)

    @pl.when(step == 0)
    def _():
        prep = 512

        def prep_body(r, carry):
            rows = pl.ds(pl.multiple_of(r * prep, prep), prep)
            kn_scr[rows, :] = _pair_rms(k_ref[rows, :].astype(f32), kg_ref[...], group_mean).astype(bf16)
            return carry

        lax.fori_loop(0, seq // prep, prep_body, 0)

    q_scale = math.log2(math.e) / math.sqrt(SB_DH)
    qn = _pair_rms(q_ref[...].astype(f32), qg_ref[...], group_mean) * q_scale
    qh = [jnp.where(m, qn, 0.0).astype(bf16) for m in head_mask]

    row = lax.broadcasted_iota(jnp.int32, (SB_T, SB_T), 0)
    col = lax.broadcasted_iota(jnp.int32, (SB_T, SB_T), 1)
    earlier = col < row
    urow = lax.broadcasted_iota(jnp.int32, (2 * SB_T, 2 * SB_T), 0) % SB_T
    ucol = lax.broadcasted_iota(jnp.int32, (2 * SB_T, 2 * SB_T), 1)
    suffix2 = jnp.where((urow >= ucol) | (ucol >= SB_T), 1.0, 0.0).astype(bf16)

    cmin = None
    for c in range(n_sub):
        qi = step * n_sub + c
        q_rows = slice(c * SB_T, (c + 1) * SB_T)
        if c == 0:
            first = qi == 0
            k0 = jnp.where(first, 0, qi - 1) * SB_T
            off = jnp.where(first, 0, SB_T)
            ok_l = col < row + off
            ok_d = col + SB_T < row + off
        else:
            k0 = (qi - 1) * SB_T
            ok_l, ok_d = None, earlier
        keys = pl.ds(pl.multiple_of(k0, SB_T), 2 * SB_T)
        kk = kn_scr[keys, :]
        vv = v_ref[keys, :]
        for h in range(2):
            z = _dot_nt(qh[h][q_rows, :], kk)
            zl, zd = z[:, :SB_T], z[:, SB_T:]
            zd = jnp.where(ok_d, zd, NEG_INF)
            if ok_l is not None:
                zl = jnp.where(ok_l, zl, NEG_INF)
            rd, car = _suffix_sums(_softplus2(zd), suffix2)
            rl, tot_l = _suffix_sums(_softplus2(zl), suffix2)
            rl = rl + car
            car = car + tot_l
            a = jnp.concatenate([jnp.exp2(zl - rl), jnp.exp2(zd - rd)], axis=1)
            acc_scr[2 * c + h] = _dot(a.astype(bf16), vv)
            car_scr[2 * c + h] = car
            if c >= 2:
                live = car
            else:
                live = jnp.where(qi >= 2, car, jnp.inf)
            cmin = live if cmin is None else jnp.minimum(cmin, live)

    def any_live(m):
        m = jnp.min(jnp.min(m, axis=1, keepdims=True), axis=0, keepdims=True)
        return m[0, 0] <= SB_DONE_LOG2

    def more(state):
        return state[1]

    def walk(state):
        d = state[0]
        cmin = None
        for c in range(n_sub):
            kb = step * n_sub + c - d
            valid = kb >= 0
            keys = pl.ds(pl.multiple_of(jnp.maximum(kb, 0) * SB_T, SB_T), SB_T)
            kk = kn_scr[keys, :]
            vv = v_ref[keys, :]
            for h in range(2):
                z = jnp.where(valid, _dot_nt(qh[h][c * SB_T:(c + 1) * SB_T, :], kk), NEG_INF)
                r, tot = _suffix_sums(_softplus2(z), suffix2)
                car = car_scr[2 * c + h]
                a = jnp.exp2(z - (r + car))
                acc_scr[2 * c + h] += _dot(a.astype(bf16), vv)
                car = car + tot
                car_scr[2 * c + h] = car
                live = jnp.where(kb >= 1, car, jnp.inf)
                cmin = live if cmin is None else jnp.minimum(cmin, live)
        return d + 1, any_live(cmin)

    lax.while_loop(more, walk, (jnp.int32(2), any_live(cmin)))

    for c in range(n_sub):
        o_ref[c * SB_T:(c + 1) * SB_T, :] = jnp.where(
            lo_mask, acc_scr[2 * c], acc_scr[2 * c + 1]).astype(o_ref.dtype)


def _sb_attn(proj, q_gain, k_gain, batch, seq):
    t = proj.shape[0]
    hpg = D_MODEL // LANES
    n_pairs = SB_HEADS // 2
    n_qt = seq // SB_QROWS
    n_chains = 2 * (SB_QROWS // SB_T)
    return pl.pallas_call(
        _sb_kernel,
        grid=(batch, n_pairs, n_qt),
        in_specs=[
            pl.BlockSpec((SB_QROWS, LANES), lambda b, p, i: (b * n_qt + i, G_SQ * hpg + p)),
            pl.BlockSpec((seq, LANES), lambda b, p, i: (b, G_SK * hpg + p)),
            pl.BlockSpec((seq, LANES), lambda b, p, i: (b, G_SV * hpg + p)),
            pl.BlockSpec((None, 1, LANES), lambda b, p, i: (p, 0, 0)),
            pl.BlockSpec((None, 1, LANES), lambda b, p, i: (p, 0, 0)),
        ],
        out_specs=pl.BlockSpec((SB_QROWS, LANES), lambda b, p, i: (b * n_qt + i, p)),
        out_shape=jax.ShapeDtypeStruct((t, SB_HEADS * SB_DH), bf16),
        scratch_shapes=[
            pltpu.VMEM((seq, LANES), bf16),
            pltpu.VMEM((n_chains, SB_T, LANES), f32),
            pltpu.VMEM((n_chains, SB_T, LANES), f32),
        ],
        compiler_params=pltpu.CompilerParams(
            dimension_semantics=("arbitrary", "arbitrary", "arbitrary"), vmem_limit_bytes=VMEM_LIMIT),
        name="sb_attn",
    )(proj, proj, proj, q_gain.reshape(n_pairs, 1, LANES), k_gain.reshape(n_pairs, 1, LANES))


def _merge_kernel(x_ref, ohg_ref, osb_ref, ga_ref, gb_ref, whg_ref, wsb_ref, wo_ref, o_ref):
    y_hg = _dot(ohg_ref[...], whg_ref[...])
    y_sb = _dot(osb_ref[...], wsb_ref[...])
    mixed = (_sigmoid(ga_ref[...].astype(f32)) * y_hg
             + _sigmoid(gb_ref[...].astype(f32)) * y_sb)
    o_ref[...] = x_ref[...] + _dot(mixed.astype(bf16), wo_ref[...])


def _merge(x2, o_hg, o_sb, proj, w_hg, w_sb, w_o, tm=512):
    t = x2.shape[0]
    rows = lambda i: (i, 0)
    whole = lambda i: (0, 0)
    return pl.pallas_call(
        _merge_kernel,
        grid=(t // tm,),
        in_specs=[
            pl.BlockSpec((tm, D_MODEL), rows),
            pl.BlockSpec((tm, D_MODEL), rows),
            pl.BlockSpec((tm, D_MODEL), rows),
            pl.BlockSpec((tm, D_MODEL), lambda i: (i, G_GA)),
            pl.BlockSpec((tm, D_MODEL), lambda i: (i, G_GB)),
            pl.BlockSpec((D_MODEL, D_MODEL), whole),
            pl.BlockSpec((D_MODEL, D_MODEL), whole),
            pl.BlockSpec((D_MODEL, D_MODEL), whole),
        ],
        out_specs=pl.BlockSpec((tm, D_MODEL), rows),
        out_shape=jax.ShapeDtypeStruct((t, D_MODEL), f32),
        compiler_params=pltpu.CompilerParams(
            dimension_semantics=("arbitrary",), vmem_limit_bytes=VMEM_LIMIT),
        name="merge",
    )(x2, o_hg, o_sb, proj, proj, w_hg, w_sb, w_o)


def _ffn_kernel(x_ref, gain_ref, wg_ref, wu_ref, wd_ref, o_ref, h_scr):
    @pl.when(pl.program_id(1) == 0)
    def _():
        x = x_ref[...]
        ms = jnp.mean(x * x, axis=-1, keepdims=True)
        h_scr[...] = (x * lax.rsqrt(ms + EPS) * gain_ref[...]).astype(bf16)
        o_ref[...] = x

    h = h_scr[...]
    gate = _dot(h, wg_ref[...])
    up = _dot(h, wu_ref[...])
    act = (gate * _sigmoid(gate) * up).astype(bf16)
    o_ref[...] += _dot(act, wd_ref[...])


def _ffn(x1, gain, w_in_bf, w_out_bf, tm=1024, th=256):
    t = x1.shape[0]
    n_h = FF_HIDDEN // th
    return pl.pallas_call(
        _ffn_kernel,
        grid=(t // tm, n_h),
        in_specs=[
            pl.BlockSpec((tm, D_MODEL), lambda i, j: (i, 0)),
            pl.BlockSpec((1, D_MODEL), lambda i, j: (0, 0)),
            pl.BlockSpec((D_MODEL, th), lambda i, j: (0, j)),
            pl.BlockSpec((D_MODEL, th), lambda i, j: (0, n_h + j)),
            pl.BlockSpec((th, D_MODEL), lambda i, j: (j, 0)),
        ],
        out_specs=pl.BlockSpec((tm, D_MODEL), lambda i, j: (i, 0)),
        out_shape=jax.ShapeDtypeStruct((t, D_MODEL), f32),
        scratch_shapes=[pltpu.VMEM((tm, D_MODEL), bf16)],
        compiler_params=pltpu.CompilerParams(
            dimension_semantics=("arbitrary", "arbitrary"), vmem_limit_bytes=VMEM_LIMIT),
        name="ffn",
    )(x1, gain, w_in_bf, w_in_bf, w_out_bf)


def kernel(x, norm1_gain, w_in, lb_logits, hg_out_norm, sb_q_norm, sb_k_norm,
           w_hg_out, w_sb_out, w_o, norm2_gain, w_ffn_in, w_ffn_out):
    batch, seq, d = x.shape
    depth = norm1_gain.shape[0]
    x2 = x.reshape(batch * seq, d)
    for layer in range(depth):
        proj, hf = _in_proj(x2, norm1_gain[layer][None, :], w_in[layer].astype(bf16))
        o_hg = _hgrn2(proj, hf, lb_logits, hg_out_norm[layer], layer, batch, seq)
        o_sb = _sb_attn(proj, sb_q_norm[layer], sb_k_norm[layer], batch, seq)
        x1 = _merge(x2, o_hg, o_sb, proj, w_hg_out[layer].astype(bf16),
                    w_sb_out[layer].astype(bf16), w_o[layer].astype(bf16))
        x2 = _ffn(x1, norm2_gain[layer][None, :], w_ffn_in[layer].astype(bf16),
                  w_ffn_out[layer].astype(bf16))
    return x2.reshape(batch, seq, d)
```

```python
import functools
import math

import jax
import jax.numpy as jnp
from jax import lax
from jax.experimental import pallas as pl
from jax.experimental.pallas import tpu as pltpu

D_MODEL = 1024
HG_HEADS = 8
HG_DK = 128
HG_DV = 128
SB_HEADS = 16
SB_DH = 64
FF_HIDDEN = 2816
EPS = 1e-6
G_HQ, G_HF, G_HI, G_HG, G_SQ, G_SK, G_SV, G_GA, G_GB = range(9)

LANES = 128
HG_CHUNK = 64
HG_ROWS = 512
HG_EXP_CLAMP = 80.0
SB_T = 128
SB_QROWS = 512
SB_DONE_LOG2 = 128.0
VMEM_LIMIT = 56 * 1024 * 1024

f32 = jnp.float32
bf16 = jnp.bfloat16
NEG_INF = float("-inf")


def _sigmoid(x):
    return 1.0 / (1.0 + jnp.exp(-x))


def _dot(a, b):
    return jnp.dot(a, b, preferred_element_type=f32)


def _dot_nt(a, b):
    return lax.dot_general(a, b, (((1,), (1,)), ((), ())), preferred_element_type=f32)


def _dot_tn(a, b):
    return lax.dot_general(a, b, (((0,), (0,)), ((), ())), preferred_element_type=f32)


def _split2(x):
    hi = x.astype(bf16)
    lo = (x - hi.astype(f32)).astype(bf16)
    return hi, lo


def _split3(g):
    g1 = g.astype(bf16)
    r1 = g - g1.astype(f32)
    g2 = r1.astype(bf16)
    g3 = (r1 - g2.astype(f32)).astype(bf16)
    return g1, g2, g3


def _inproj_kernel(x_ref, gain_ref, w_ref, o_ref, f_ref, h_scr):
    j = pl.program_id(1)

    @pl.when(j == 0)
    def _():
        x = x_ref[...]
        ms = jnp.mean(x * x, axis=-1, keepdims=True)
        h_scr[...] = (x * lax.rsqrt(ms + EPS) * gain_ref[...]).astype(bf16)

    acc = _dot(h_scr[...], w_ref[...])
    o_ref[...] = acc.astype(o_ref.dtype)

    @pl.when(j == G_HF)
    def _():
        f_ref[...] = acc


def _in_proj(x2, gain, w_bf, tm=1024):
    t = x2.shape[0]
    n = w_bf.shape[1]
    return pl.pallas_call(
        _inproj_kernel,
        grid=(t // tm, n // D_MODEL),
        in_specs=[
            pl.BlockSpec((tm, D_MODEL), lambda i, j: (i, 0)),
            pl.BlockSpec((1, D_MODEL), lambda i, j: (0, 0)),
            pl.BlockSpec((D_MODEL, D_MODEL), lambda i, j: (0, j)),
        ],
        out_specs=[
            pl.BlockSpec((tm, D_MODEL), lambda i, j: (i, j)),
            pl.BlockSpec((tm, D_MODEL), lambda i, j: (i, 0)),
        ],
        out_shape=[
            jax.ShapeDtypeStruct((t, n), bf16),
            jax.ShapeDtypeStruct((t, D_MODEL), f32),
        ],
        scratch_shapes=[pltpu.VMEM((tm, D_MODEL), bf16)],
        compiler_params=pltpu.CompilerParams(
            dimension_semantics=("arbitrary", "arbitrary"), vmem_limit_bytes=VMEM_LIMIT),
        name="in_proj",
    )(x2, gain, w_bf)


def _hgrn_kernel(q_ref, f_ref, i_ref, og_ref, lbl_ref, gain_ref, o_ref, st_scr, *, layer):
    c_len = HG_CHUNK
    n_chunks = q_ref.shape[0] // c_len

    @pl.when(pl.program_id(1) == 0)
    def _():
        st_scr[...] = jnp.zeros_like(st_scr)

    l = lbl_ref[...]
    e = jnp.exp(l - jnp.max(l, axis=0, keepdims=True))
    lb = jnp.sum(e[0:layer + 1, :], axis=0, keepdims=True) / jnp.sum(e, axis=0, keepdims=True)
    gain = gain_ref[...]

    row = lax.broadcasted_iota(jnp.int32, (c_len, c_len), 0)
    col = lax.broadcasted_iota(jnp.int32, (c_len, c_len), 1)
    causal = row >= col
    tri = jnp.where(causal, 1.0, 0.0).astype(bf16)
    tri3 = jnp.concatenate([tri, tri, tri], axis=1)

    def body(c, carry):
        rows = pl.ds(pl.multiple_of(c * c_len, c_len), c_len)
        f = lb + (1.0 - lb) * _sigmoid(f_ref[rows, :])
        g = jnp.log(f)
        k = 1.0 - f
        cum = _dot(tri3, jnp.concatenate(_split3(g), axis=0))
        last = cum[c_len - 1:c_len, :]
        mid = cum[c_len // 2 - 1:c_len // 2, :]
        q = q_ref[rows, :].astype(f32)
        qt = (q * jnp.exp(jnp.minimum(cum - mid, HG_EXP_CLAMP))).astype(bf16)
        kt = (k * jnp.exp(jnp.minimum(mid - cum, HG_EXP_CLAMP))).astype(bf16)
        qd = (q * jnp.exp(cum)).astype(bf16)
        kd = (k * jnp.exp(last - cum)).astype(bf16)
        e_last = jnp.exp(last)
        v = i_ref[rows, :]
        out_scale = gain * _sigmoid(og_ref[rows, :].astype(f32))

        heads = range(HG_HEADS)
        sl = [slice(h * HG_DK, (h + 1) * HG_DK) for h in heads]
        states = [st_scr[h] for h in heads]
        inter = [_dot_nt(qd[:, sl[h]], states[h].astype(bf16)) for h in heads]
        scores = [jnp.where(causal, _dot_nt(qt[:, sl[h]], kt[:, sl[h]]), 0.0).astype(bf16) for h in heads]
        upd = [_dot_tn(v[:, sl[h]], kd[:, sl[h]]) for h in heads]
        o = [_dot(scores[h], v[:, sl[h]]) + inter[h] for h in heads]
        outs = []
        for h in heads:
            st_scr[h] = states[h] * e_last[:, sl[h]] + upd[h]
            ms = jnp.mean(o[h] * o[h], axis=-1, keepdims=True)
            outs.append((o[h] * lax.rsqrt(ms + EPS) * out_scale[:, sl[h]]).astype(o_ref.dtype))
        o_ref[rows, :] = jnp.concatenate(outs, axis=1)
        return carry

    lax.fori_loop(0, n_chunks, body, 0)


def _hgrn2(proj, hf, lb_logits, hg_gain, layer, batch, seq):
    t = proj.shape[0]
    n_rt = seq // HG_ROWS

    def grp(group):
        return pl.BlockSpec((HG_ROWS, D_MODEL), lambda b, r: (b * n_rt + r, group))

    return pl.pallas_call(
        functools.partial(_hgrn_kernel, layer=layer),
        grid=(batch, n_rt),
        in_specs=[
            grp(G_HQ), grp(0), grp(G_HI), grp(G_HG),
            pl.BlockSpec(lb_logits.shape, lambda b, r: (0, 0)),
            pl.BlockSpec((1, D_MODEL), lambda b, r: (0, 0)),
        ],
        out_specs=grp(0),
        out_shape=jax.ShapeDtypeStruct((t, HG_HEADS * HG_DV), bf16),
        scratch_shapes=[pltpu.VMEM((HG_HEADS, HG_DV, HG_DK), f32)],
        compiler_params=pltpu.CompilerParams(
            dimension_semantics=("arbitrary", "arbitrary"), vmem_limit_bytes=VMEM_LIMIT),
        name="hgrn2",
    )(proj, hf, proj, proj, lb_logits, hg_gain.reshape(1, HG_HEADS * HG_DV))


def _pair_rms(x, gain, group_mean):
    hi, lo = _split2(x * x)
    ms = _dot(hi, group_mean) + _dot(lo, group_mean)
    return x * lax.rsqrt(ms + EPS) * gain


def _softplus2(z):
    return jnp.maximum(z, 0.0) + jnp.log2(1.0 + jnp.exp2(-jnp.abs(z)))


def _hi_lo(sp):
    return jnp.concatenate(_split2(sp), axis=1)


def _sb_kernel(q_ref, k_ref, v_ref, qg_ref, kg_ref, o_ref, kn_scr, q2_scr, acc_scr, car_scr):
    step = pl.program_id(2)
    seq = k_ref.shape[0]
    n_sub = SB_QROWS // SB_T
    lane = lax.broadcasted_iota(jnp.int32, (1, LANES), 1)
    lo_mask = lane < SB_DH

    gr = lax.broadcasted_iota(jnp.int32, (LANES, LANES), 0) // SB_DH
    gc = lax.broadcasted_iota(jnp.int32, (LANES, LANES), 1) // SB_DH
    group_mean = jnp.where(gr == gc, 1.0 / SB_DH, 0.0).astype(bf16)

    @pl.when(step == 0)
    def _():
        prep = 512

        def prep_body(r, carry):
            rows = pl.ds(pl.multiple_of(r * prep, prep), prep)
            kn_scr[rows, :] = _pair_rms(k_ref[rows, :].astype(f32), kg_ref[...], group_mean).astype(bf16)
            return carry

        lax.fori_loop(0, seq // prep, prep_body, 0)

    q_scale = math.log2(math.e) / math.sqrt(SB_DH)
    qn = _pair_rms(q_ref[...].astype(f32), qg_ref[...], group_mean) * q_scale
    q_h0 = jnp.where(lo_mask, qn, 0.0).astype(bf16)
    q_h1 = jnp.where(lo_mask, 0.0, qn).astype(bf16)
    q2 = [jnp.concatenate([q_h0[c * SB_T:(c + 1) * SB_T, :], q_h1[c * SB_T:(c + 1) * SB_T, :]], axis=0)
          for c in range(n_sub)]

    row = lax.broadcasted_iota(jnp.int32, (2 * SB_T, SB_T), 0) % SB_T
    col = lax.broadcasted_iota(jnp.int32, (2 * SB_T, SB_T), 1)
    earlier = col < row
    urow = lax.broadcasted_iota(jnp.int32, (2 * SB_T, 2 * SB_T), 0) % SB_T
    ucol = lax.broadcasted_iota(jnp.int32, (2 * SB_T, 2 * SB_T), 1)
    suffix2 = jnp.where((urow >= ucol) | (ucol >= SB_T), 1.0, 0.0).astype(bf16)

    first = step == 0
    k_start = [jnp.where(first, 0, step * n_sub - 1) * SB_T] + [
        (step * n_sub + c - 1) * SB_T for c in range(1, n_sub)]
    pair_keys = [pl.ds(pl.multiple_of(k0, SB_T), 2 * SB_T) for k0 in k_start]

    zs = [_dot_nt(q2[c], kn_scr[pair_keys[c], :]) for c in range(n_sub)]
    off = jnp.where(first, 0, SB_T)
    zl = [zs[c][:, :SB_T] for c in range(n_sub)]
    zd = [jnp.where(col + SB_T < row + off if c == 0 else earlier, zs[c][:, SB_T:], NEG_INF)
          for c in range(n_sub)]
    zl[0] = jnp.where(col < row + off, zl[0], NEG_INF)
    fulls = [_dot(jnp.concatenate([_hi_lo(_softplus2(zd[c])), _hi_lo(_softplus2(zl[c]))], axis=0), suffix2)
             for c in range(n_sub)]
    cmin = None
    accs, cars = [], []
    for c in range(n_sub):
        full = fulls[c]
        rd, car = full[:2 * SB_T, :SB_T], full[:2 * SB_T, SB_T:]
        rl = full[2 * SB_T:, :SB_T] + car
        car = car + full[2 * SB_T:, SB_T:]
        a = jnp.concatenate([jnp.exp2(zl[c] - rl), jnp.exp2(zd[c] - rd)], axis=1)
        accs.append(_dot(a.astype(bf16), v_ref[pair_keys[c], :]))
        cars.append(car)
        if c >= 2:
            live = car
        else:
            live = jnp.where(step * n_sub + c >= 2, car, jnp.inf)
        cmin = live if cmin is None else jnp.minimum(cmin, live)
    for c in range(n_sub):
        q2_scr[c] = q2[c]
        acc_scr[c] = accs[c]
        car_scr[c] = cars[c]

    def any_live(m):
        m = jnp.min(jnp.min(m, axis=1, keepdims=True), axis=0, keepdims=True)
        return m[0, 0] <= SB_DONE_LOG2

    def more(state):
        return state[1]

    def walk(state):
        d = state[0]
        kbs = [step * n_sub + c - d for c in range(n_sub)]
        tile_keys = [pl.ds(pl.multiple_of(jnp.maximum(kb, 0) * SB_T, SB_T), SB_T) for kb in kbs]
        zs = [jnp.where(kbs[c] >= 0, _dot_nt(q2_scr[c], kn_scr[tile_keys[c], :]), NEG_INF)
              for c in range(n_sub)]
        fulls = [_dot(_hi_lo(_softplus2(zs[c])), suffix2) for c in range(n_sub)]
        cmin = None
        accs, cars = [], []
        for c in range(n_sub):
            car = car_scr[c]
            a = jnp.exp2(zs[c] - (fulls[c][:, :SB_T] + car))
            accs.append(acc_scr[c] + _dot(a.astype(bf16), v_ref[tile_keys[c], :]))
            car = car + fulls[c][:, SB_T:]
            cars.append(car)
            live = jnp.where(kbs[c] >= 1, car, jnp.inf)
            cmin = live if cmin is None else jnp.minimum(cmin, live)
        for c in range(n_sub):
            acc_scr[c] = accs[c]
            car_scr[c] = cars[c]
        return d + 1, any_live(cmin)

    lax.while_loop(more, walk, (jnp.int32(2), any_live(cmin)))

    for c in range(n_sub):
        o_ref[c * SB_T:(c + 1) * SB_T, :] = jnp.where(
            lo_mask, acc_scr[c, :SB_T, :], acc_scr[c, SB_T:, :]).astype(o_ref.dtype)


def _sb_attn(proj, q_gain, k_gain, batch, seq):
    t = proj.shape[0]
    hpg = D_MODEL // LANES
    n_pairs = SB_HEADS // 2
    n_qt = seq // SB_QROWS
    n_sub = SB_QROWS // SB_T
    return pl.pallas_call(
        _sb_kernel,
        grid=(batch, n_pairs, n_qt),
        in_specs=[
            pl.BlockSpec((SB_QROWS, LANES), lambda b, p, i: (b * n_qt + i, G_SQ * hpg + p)),
            pl.BlockSpec((seq, LANES), lambda b, p, i: (b, G_SK * hpg + p)),
            pl.BlockSpec((seq, LANES), lambda b, p, i: (b, G_SV * hpg + p)),
            pl.BlockSpec((None, 1, LANES), lambda b, p, i: (p, 0, 0)),
            pl.BlockSpec((None, 1, LANES), lambda b, p, i: (p, 0, 0)),
        ],
        out_specs=pl.BlockSpec((SB_QROWS, LANES), lambda b, p, i: (b * n_qt + i, p)),
        out_shape=jax.ShapeDtypeStruct((t, SB_HEADS * SB_DH), bf16),
        scratch_shapes=[
            pltpu.VMEM((seq, LANES), bf16),
            pltpu.VMEM((n_sub, 2 * SB_T, LANES), bf16),
            pltpu.VMEM((n_sub, 2 * SB_T, LANES), f32),
            pltpu.VMEM((n_sub, 2 * SB_T, LANES), f32),
        ],
        compiler_params=pltpu.CompilerParams(
            dimension_semantics=("arbitrary", "arbitrary", "arbitrary"), vmem_limit_bytes=VMEM_LIMIT),
        name="sb_attn",
    )(proj, proj, proj, q_gain.reshape(n_pairs, 1, LANES), k_gain.reshape(n_pairs, 1, LANES))


def _merge_kernel(x_ref, ohg_ref, osb_ref, ga_ref, gb_ref, whg_ref, wsb_ref, wo_ref, o_ref):
    y_hg = _dot(ohg_ref[...], whg_ref[...])
    y_sb = _dot(osb_ref[...], wsb_ref[...])
    mixed = (_sigmoid(ga_ref[...].astype(f32)) * y_hg
             + _sigmoid(gb_ref[...].astype(f32)) * y_sb)
    o_ref[...] = x_ref[...] + _dot(mixed.astype(bf16), wo_ref[...])


def _merge(x2, o_hg, o_sb, proj, w_hg, w_sb, w_o, tm=512):
    t = x2.shape[0]
    rows = lambda i: (i, 0)
    whole = lambda i: (0, 0)
    return pl.pallas_call(
        _merge_kernel,
        grid=(t // tm,),
        in_specs=[
            pl.BlockSpec((tm, D_MODEL), rows),
            pl.BlockSpec((tm, D_MODEL), rows),
            pl.BlockSpec((tm, D_MODEL), rows),
            pl.BlockSpec((tm, D_MODEL), lambda i: (i, G_GA)),
            pl.BlockSpec((tm, D_MODEL), lambda i: (i, G_GB)),
            pl.BlockSpec((D_MODEL, D_MODEL), whole),
            pl.BlockSpec((D_MODEL, D_MODEL), whole),
            pl.BlockSpec((D_MODEL, D_MODEL), whole),
        ],
        out_specs=pl.BlockSpec((tm, D_MODEL), rows),
        out_shape=jax.ShapeDtypeStruct((t, D_MODEL), f32),
        compiler_params=pltpu.CompilerParams(
            dimension_semantics=("arbitrary",), vmem_limit_bytes=VMEM_LIMIT),
        name="merge",
    )(x2, o_hg, o_sb, proj, proj, w_hg, w_sb, w_o)


def _ffn_kernel(x_ref, gain_ref, wg_ref, wu_ref, wd_ref, o_ref, h_scr):
    @pl.when(pl.program_id(1) == 0)
    def _():
        x = x_ref[...]
        ms = jnp.mean(x * x, axis=-1, keepdims=True)
        h_scr[...] = (x * lax.rsqrt(ms + EPS) * gain_ref[...]).astype(bf16)
        o_ref[...] = x

    h = h_scr[...]
    gate = _dot(h, wg_ref[...])
    up = _dot(h, wu_ref[...])
    act = (gate * _sigmoid(gate) * up).astype(bf16)
    o_ref[...] += _dot(act, wd_ref[...])


def _ffn(x1, gain, w_in_bf, w_out_bf, tm=2048, th=256):
    t = x1.shape[0]
    n_h = FF_HIDDEN // th
    return pl.pallas_call(
        _ffn_kernel,
        grid=(t // tm, n_h),
        in_specs=[
            pl.BlockSpec((tm, D_MODEL), lambda i, j: (i, 0)),
            pl.BlockSpec((1, D_MODEL), lambda i, j: (0, 0)),
            pl.BlockSpec((D_MODEL, th), lambda i, j: (0, j)),
            pl.BlockSpec((D_MODEL, th), lambda i, j: (0, n_h + j)),
            pl.BlockSpec((th, D_MODEL), lambda i, j: (j, 0)),
        ],
        out_specs=pl.BlockSpec((tm, D_MODEL), lambda i, j: (i, 0)),
        out_shape=jax.ShapeDtypeStruct((t, D_MODEL), f32),
        scratch_shapes=[pltpu.VMEM((tm, D_MODEL), bf16)],
        compiler_params=pltpu.CompilerParams(
            dimension_semantics=("arbitrary", "arbitrary"), vmem_limit_bytes=VMEM_LIMIT),
        name="ffn",
    )(x1, gain, w_in_bf, w_in_bf, w_out_bf)


def kernel(x, norm1_gain, w_in, lb_logits, hg_out_norm, sb_q_norm, sb_k_norm,
           w_hg_out, w_sb_out, w_o, norm2_gain, w_ffn_in, w_ffn_out):
    batch, seq, d = x.shape
    depth = norm1_gain.shape[0]
    x2 = x.reshape(batch * seq, d)
    for layer in range(depth):
        proj, hf = _in_proj(x2, norm1_gain[layer][None, :], w_in[layer].astype(bf16))
        o_hg = _hgrn2(proj, hf, lb_logits, hg_out_norm[layer], layer, batch, seq)
        o_sb = _sb_attn(proj, sb_q_norm[layer], sb_k_norm[layer], batch, seq)
        x1 = _merge(x2, o_hg, o_sb, proj, w_hg_out[layer].astype(bf16),
                    w_sb_out[layer].astype(bf16), w_o[layer].astype(bf16))
        x2 = _ffn(x1, norm2_gain[layer][None, :], w_ffn_in[layer].astype(bf16),
                  w_ffn_out[layer].astype(bf16))
    return x2.reshape(batch, seq, d)
```

```python
import functools
import math

import jax
import jax.numpy as jnp
from jax import lax
from jax.experimental import pallas as pl
from jax.experimental.pallas import tpu as pltpu

D_MODEL = 1024
HG_HEADS = 8
HG_DK = 128
HG_DV = 128
SB_HEADS = 16
SB_DH = 64
FF_HIDDEN = 2816
FF_CHUNK = 256
EPS = 1e-6
G_HQ, G_HF, G_HI, G_HG, G_SQ, G_SK, G_SV, G_GA, G_GB = range(9)
P_SOURCE = (G_HQ, G_HI, G_HG, G_SQ, G_SK, G_SV, G_GA, G_GB)
P_HQ, P_HI, P_HG, P_SQ, P_SK, P_SV, P_GA, P_GB = range(8)

LANES = 128
HG_CHUNK = 64
HG_PAR = 4
HG_ROWS = 512
HG_EXP_CLAMP = 80.0
SB_T = 128
SB_QROWS = 1024
SB_DONE_LOG2 = 128.0
VMEM_LIMIT = 56 * 1024 * 1024

f32 = jnp.float32
bf16 = jnp.bfloat16
NEG_INF = float("-inf")


def _sigmoid(x):
    return 1.0 / (1.0 + jnp.exp(-x))


def _sigmoid_tanh(x):
    return 0.5 * jnp.tanh(0.5 * x) + 0.5


def _dot(a, b):
    return jnp.dot(a, b, preferred_element_type=f32)


def _dot_nt(a, b):
    return lax.dot_general(a, b, (((1,), (1,)), ((), ())), preferred_element_type=f32)


def _dot_tn(a, b):
    return lax.dot_general(a, b, (((0,), (0,)), ((), ())), preferred_element_type=f32)


def _split2(x):
    hi = x.astype(bf16)
    lo = (x - hi.astype(f32)).astype(bf16)
    return hi, lo


def _rms_norm_bf16(x, gain):
    ms = jnp.mean(x * x, axis=-1, keepdims=True)
    return (x * lax.rsqrt(ms + EPS) * gain).astype(bf16)


def _inproj_kernel(x_ref, gain_ref, w_ref, o_ref, h_scr):
    j = pl.program_id(1)

    @pl.when(j == 0)
    def _():
        h_scr[...] = _rms_norm_bf16(x_ref[...], gain_ref[...])

    is_gate = (j == P_HG) | (j >= P_GA)

    @pl.when(is_gate)
    def _():
        o_ref[...] = _sigmoid_tanh(_dot(h_scr[...], w_ref[...].astype(bf16))).astype(o_ref.dtype)

    @pl.when(jnp.logical_not(is_gate))
    def _():
        o_ref[...] = _dot(h_scr[...], w_ref[...].astype(bf16)).astype(o_ref.dtype)


def _in_proj(x2, gain, w_in, tm=2048):
    t = x2.shape[0]
    n_groups = len(P_SOURCE)
    return pl.pallas_call(
        _inproj_kernel,
        grid=(t // tm, n_groups),
        in_specs=[
            pl.BlockSpec((tm, D_MODEL), lambda i, j: (i, 0)),
            pl.BlockSpec((1, D_MODEL), lambda i, j: (0, 0)),
            pl.BlockSpec((D_MODEL, D_MODEL), lambda i, j: (0, jnp.where(j >= G_HF, j + 1, j))),
        ],
        out_specs=pl.BlockSpec((tm, D_MODEL), lambda i, j: (i, j)),
        out_shape=jax.ShapeDtypeStruct((t, n_groups * D_MODEL), bf16),
        scratch_shapes=[pltpu.VMEM((tm, D_MODEL), bf16)],
        compiler_params=pltpu.CompilerParams(
            dimension_semantics=("arbitrary", "arbitrary"), vmem_limit_bytes=VMEM_LIMIT),
        name="in_proj",
    )(x2, gain, w_in)


def _forget_kernel(x_ref, gain_ref, w_ref, lbl_ref, g_ref, k_ref, *, layer):
    l = lbl_ref[...]
    e = jnp.exp(l - jnp.max(l, axis=0, keepdims=True))
    lb = jnp.sum(e[0:layer + 1, :], axis=0, keepdims=True) / jnp.sum(e, axis=0, keepdims=True)
    f_mid = 0.5 * (1.0 + lb)
    f_amp = 0.5 * (1.0 - lb)
    w = w_ref[...].astype(bf16)
    n_blk = 4
    blk = x_ref.shape[0] // n_blk
    ys = [_dot(_rms_norm_bf16(x_ref[b * blk:(b + 1) * blk, :], gain_ref[...]), w) for b in range(n_blk)]
    for b in range(n_blk):
        f = f_mid + f_amp * jnp.tanh(0.5 * ys[b])
        g_ref[b * blk:(b + 1) * blk, :] = jnp.log(f)
        k_ref[b * blk:(b + 1) * blk, :] = (1.0 - f).astype(k_ref.dtype)


def _forget_proj(x2, gain, w_in, lb_logits, layer, tm=1024):
    t = x2.shape[0]
    rows = lambda i: (i, 0)
    return pl.pallas_call(
        functools.partial(_forget_kernel, layer=layer),
        grid=(t // tm,),
        in_specs=[
            pl.BlockSpec((tm, D_MODEL), rows),
            pl.BlockSpec((1, D_MODEL), lambda i: (0, 0)),
            pl.BlockSpec((D_MODEL, D_MODEL), lambda i: (0, G_HF)),
            pl.BlockSpec(lb_logits.shape, lambda i: (0, 0)),
        ],
        out_specs=[pl.BlockSpec((tm, D_MODEL), rows), pl.BlockSpec((tm, D_MODEL), rows)],
        out_shape=[jax.ShapeDtypeStruct((t, D_MODEL), f32), jax.ShapeDtypeStruct((t, D_MODEL), bf16)],
        compiler_params=pltpu.CompilerParams(
            dimension_semantics=("arbitrary",), vmem_limit_bytes=VMEM_LIMIT),
        name="forget_proj",
    )(x2, gain, w_in, lb_logits)


def _hgrn_kernel(q_ref, g_ref, k_ref, i_ref, og_ref, gain_ref, o_ref, st_scr):
    c_len = HG_CHUNK
    n_chunks = q_ref.shape[0] // c_len

    @pl.when(pl.program_id(1) == 0)
    def _():
        st_scr[...] = jnp.zeros_like(st_scr)

    gain = gain_ref[...]

    row = lax.broadcasted_iota(jnp.int32, (c_len, c_len), 0)
    col = lax.broadcasted_iota(jnp.int32, (c_len, c_len), 1)
    causal = row >= col
    tri = jnp.where(causal, 1.0, 0.0).astype(bf16)
    tri2 = jnp.concatenate([tri, tri], axis=1)
    heads = range(HG_HEADS)
    sl = [slice(h * HG_DK, (h + 1) * HG_DK) for h in heads]

    def chunk_operands(c):
        rows = pl.ds(pl.multiple_of(c * c_len, c_len), c_len)
        k = k_ref[rows, :].astype(f32)
        cum = _dot(tri2, jnp.concatenate(_split2(g_ref[rows, :]), axis=0))
        last = cum[c_len - 1:c_len, :]
        mid = cum[c_len // 2 - 1:c_len // 2, :]
        q = q_ref[rows, :].astype(f32)
        return dict(
            rows=rows,
            qt=(q * jnp.exp(jnp.minimum(cum - mid, HG_EXP_CLAMP))).astype(bf16),
            kt=(k * jnp.exp(jnp.minimum(mid - cum, HG_EXP_CLAMP))).astype(bf16),
            qd=(q * jnp.exp(cum)).astype(bf16),
            kd=(k * jnp.exp(last - cum)).astype(bf16),
            e_last=jnp.exp(last),
            v=i_ref[rows, :],
            out_scale=gain * og_ref[rows, :].astype(f32),
        )

    def body(it, carry):
        cs = [chunk_operands(it * HG_PAR + p) for p in range(HG_PAR)]
        scores = [[jnp.where(causal, _dot_nt(c["qt"][:, sl[h]], c["kt"][:, sl[h]]), 0.0).astype(bf16)
                   for h in heads] for c in cs]
        upd = [[_dot_tn(c["v"][:, sl[h]], c["kd"][:, sl[h]]) for h in heads] for c in cs]
        states = [st_scr[h] for h in heads]
        inter = []
        for p, c in enumerate(cs):
            inter.append([_dot_nt(c["qd"][:, sl[h]], states[h].astype(bf16)) for h in heads])
            states = [states[h] * c["e_last"][:, sl[h]] + upd[p][h] for h in heads]
        for h in heads:
            st_scr[h] = states[h]
        for p, c in enumerate(cs):
            outs = []
            for h in heads:
                o = _dot(scores[p][h], c["v"][:, sl[h]]) + inter[p][h]
                ms = jnp.mean(o * o, axis=-1, keepdims=True)
                outs.append((o * lax.rsqrt(ms + EPS) * c["out_scale"][:, sl[h]]).astype(o_ref.dtype))
            o_ref[c["rows"], :] = jnp.concatenate(outs, axis=1)
        return carry

    lax.fori_loop(0, n_chunks // HG_PAR, body, 0)


def _hgrn2(proj, log_f, one_minus_f, hg_gain, batch, seq):
    t = proj.shape[0]
    n_rt = seq // HG_ROWS

    def grp(group):
        return pl.BlockSpec((HG_ROWS, D_MODEL), lambda b, r: (b * n_rt + r, group))

    return pl.pallas_call(
        _hgrn_kernel,
        grid=(batch, n_rt),
        in_specs=[
            grp(P_HQ), grp(0), grp(0), grp(P_HI), grp(P_HG),
            pl.BlockSpec((1, D_MODEL), lambda b, r: (0, 0)),
        ],
        out_specs=grp(0),
        out_shape=jax.ShapeDtypeStruct((t, HG_HEADS * HG_DV), bf16),
        scratch_shapes=[pltpu.VMEM((HG_HEADS, HG_DV, HG_DK), f32)],
        compiler_params=pltpu.CompilerParams(
            dimension_semantics=("arbitrary", "arbitrary"), vmem_limit_bytes=VMEM_LIMIT),
        name="hgrn2",
    )(proj, log_f, one_minus_f, proj, proj, hg_gain.reshape(1, HG_HEADS * HG_DV))


def _pair_rms(x, gain, group_mean):
    hi, lo = _split2(x * x)
    ms = _dot(hi, group_mean) + _dot(lo, group_mean)
    return x * lax.rsqrt(ms + EPS) * gain


def _softplus2(z):
    return jnp.maximum(z, 0.0) + jnp.log2(1.0 + jnp.exp2(-jnp.abs(z)))


def _hi_lo(sp):
    return jnp.concatenate(_split2(sp), axis=1)


def _sb_kernel(q_ref, k_ref, v_ref, qg_ref, kg_ref, o_ref, kn_scr, q2_scr, acc_scr, car_scr):
    step = pl.program_id(2)
    seq = k_ref.shape[0]
    n_sub = SB_QROWS // SB_T
    lane = lax.broadcasted_iota(jnp.int32, (1, LANES), 1)
    lo_mask = lane < SB_DH

    gr = lax.broadcasted_iota(jnp.int32, (LANES, LANES), 0) // SB_DH
    gc = lax.broadcasted_iota(jnp.int32, (LANES, LANES), 1) // SB_DH
    group_mean = jnp.where(gr == gc, 1.0 / SB_DH, 0.0).astype(bf16)

    @pl.when(step == 0)
    def _():
        prep = 512

        def prep_body(r, carry):
            rows = pl.ds(pl.multiple_of(r * prep, prep), prep)
            kn_scr[rows, :] = _pair_rms(k_ref[rows, :].astype(f32), kg_ref[...], group_mean).astype(bf16)
            return carry

        lax.fori_loop(0, seq // prep, prep_body, 0)

    q_scale = math.log2(math.e) / math.sqrt(SB_DH)
    qn = _pair_rms(q_ref[...].astype(f32), qg_ref[...], group_mean) * q_scale
    q_h0 = jnp.where(lo_mask, qn, 0.0).astype(bf16)
    q_h1 = jnp.where(lo_mask, 0.0, qn).astype(bf16)
    q2 = [jnp.concatenate([q_h0[c * SB_T:(c + 1) * SB_T, :], q_h1[c * SB_T:(c + 1) * SB_T, :]], axis=0)
          for c in range(n_sub)]

    row = lax.broadcasted_iota(jnp.int32, (2 * SB_T, SB_T), 0) % SB_T
    col = lax.broadcasted_iota(jnp.int32, (2 * SB_T, SB_T), 1)
    earlier = col < row
    urow = lax.broadcasted_iota(jnp.int32, (2 * SB_T, 2 * SB_T), 0) % SB_T
    ucol = lax.broadcasted_iota(jnp.int32, (2 * SB_T, 2 * SB_T), 1)
    suffix2 = jnp.where((urow >= ucol) | (ucol >= SB_T), 1.0, 0.0).astype(bf16)

    first = step == 0
    k_start = [jnp.where(first, 0, step * n_sub - 1) * SB_T] + [
        (step * n_sub + c - 1) * SB_T for c in range(1, n_sub)]
    pair_keys = [pl.ds(pl.multiple_of(k0, SB_T), 2 * SB_T) for k0 in k_start]

    zs = [_dot_nt(q2[c], kn_scr[pair_keys[c], :]) for c in range(n_sub)]
    off = jnp.where(first, 0, SB_T)
    zl = [zs[c][:, :SB_T] for c in range(n_sub)]
    zd = [jnp.where(col + SB_T < row + off if c == 0 else earlier, zs[c][:, SB_T:], NEG_INF)
          for c in range(n_sub)]
    zl[0] = jnp.where(col < row + off, zl[0], NEG_INF)
    fulls = [_dot(jnp.concatenate([_hi_lo(_softplus2(zd[c])), _hi_lo(_softplus2(zl[c]))], axis=0), suffix2)
             for c in range(n_sub)]
    cmin = None
    accs, cars = [], []
    for c in range(n_sub):
        full = fulls[c]
        rd, car = full[:2 * SB_T, :SB_T], full[:2 * SB_T, SB_T:]
        rl = full[2 * SB_T:, :SB_T] + car
        car = car + full[2 * SB_T:, SB_T:]
        a = jnp.concatenate([jnp.exp2(zl[c] - rl), jnp.exp2(zd[c] - rd)], axis=1)
        accs.append(_dot(a.astype(bf16), v_ref[pair_keys[c], :]))
        cars.append(car)
        if c >= 2:
            live = car
        else:
            live = jnp.where(step * n_sub + c >= 2, car, jnp.inf)
        cmin = live if cmin is None else jnp.minimum(cmin, live)
    for c in range(n_sub):
        q2_scr[c] = q2[c]
        acc_scr[c] = accs[c]
        car_scr[c] = cars[c]

    def any_live(m):
        m = jnp.min(jnp.min(m, axis=1, keepdims=True), axis=0, keepdims=True)
        return m[0, 0] <= SB_DONE_LOG2

    def more(state):
        return state[1]

    def walk(state):
        d = state[0]
        kbs = [step * n_sub + c - d for c in range(n_sub)]
        tile_keys = [pl.ds(pl.multiple_of(jnp.maximum(kb, 0) * SB_T, SB_T), SB_T) for kb in kbs]
        zs = [jnp.where(kbs[c] >= 0, _dot_nt(q2_scr[c], kn_scr[tile_keys[c], :]), NEG_INF)
              for c in range(n_sub)]
        fulls = [_dot(_hi_lo(_softplus2(zs[c])), suffix2) for c in range(n_sub)]
        cmin = None
        accs, cars = [], []
        for c in range(n_sub):
            car = car_scr[c]
            a = jnp.exp2(zs[c] - (fulls[c][:, :SB_T] + car))
            accs.append(acc_scr[c] + _dot(a.astype(bf16), v_ref[tile_keys[c], :]))
            car = car + fulls[c][:, SB_T:]
            cars.append(car)
            live = jnp.where(kbs[c] >= 1, car, jnp.inf)
            cmin = live if cmin is None else jnp.minimum(cmin, live)
        for c in range(n_sub):
            acc_scr[c] = accs[c]
            car_scr[c] = cars[c]
        return d + 1, any_live(cmin)

    lax.while_loop(more, walk, (jnp.int32(2), any_live(cmin)))

    for c in range(n_sub):
        o_ref[c * SB_T:(c + 1) * SB_T, :] = jnp.where(
            lo_mask, acc_scr[c, :SB_T, :], acc_scr[c, SB_T:, :]).astype(o_ref.dtype)


def _sb_attn(proj, q_gain, k_gain, batch, seq):
    t = proj.shape[0]
    hpg = D_MODEL // LANES
    n_pairs = SB_HEADS // 2
    n_qt = seq // SB_QROWS
    n_sub = SB_QROWS // SB_T
    return pl.pallas_call(
        _sb_kernel,
        grid=(batch, n_pairs, n_qt),
        in_specs=[
            pl.BlockSpec((SB_QROWS, LANES), lambda b, p, i: (b * n_qt + i, P_SQ * hpg + p)),
            pl.BlockSpec((seq, LANES), lambda b, p, i: (b, P_SK * hpg + p)),
            pl.BlockSpec((seq, LANES), lambda b, p, i: (b, P_SV * hpg + p)),
            pl.BlockSpec((None, 1, LANES), lambda b, p, i: (p, 0, 0)),
            pl.BlockSpec((None, 1, LANES), lambda b, p, i: (p, 0, 0)),
        ],
        out_specs=pl.BlockSpec((SB_QROWS, LANES), lambda b, p, i: (b * n_qt + i, p)),
        out_shape=jax.ShapeDtypeStruct((t, SB_HEADS * SB_DH), bf16),
        scratch_shapes=[
            pltpu.VMEM((seq, LANES), bf16),
            pltpu.VMEM((n_sub, 2 * SB_T, LANES), bf16),
            pltpu.VMEM((n_sub, 2 * SB_T, LANES), f32),
            pltpu.VMEM((n_sub, 2 * SB_T, LANES), f32),
        ],
        compiler_params=pltpu.CompilerParams(
            dimension_semantics=("arbitrary", "arbitrary", "arbitrary"), vmem_limit_bytes=VMEM_LIMIT),
        name="sb_attn",
    )(proj, proj, proj, q_gain.reshape(n_pairs, 1, LANES), k_gain.reshape(n_pairs, 1, LANES))


def _merge_kernel(x_ref, ohg_ref, osb_ref, ga_ref, gb_ref, whg_ref, wsb_ref, wo_ref, o_ref):
    y_hg = _dot(ohg_ref[...], whg_ref[...])
    y_sb = _dot(osb_ref[...], wsb_ref[...])
    mixed = ga_ref[...].astype(f32) * y_hg + gb_ref[...].astype(f32) * y_sb
    o_ref[...] = x_ref[...] + _dot(mixed.astype(bf16), wo_ref[...])


def _merge(x2, o_hg, o_sb, proj, w_hg, w_sb, w_o, tm=512):
    t = x2.shape[0]
    rows = lambda i: (i, 0)
    whole = lambda i: (0, 0)
    return pl.pallas_call(
        _merge_kernel,
        grid=(t // tm,),
        in_specs=[
            pl.BlockSpec((tm, D_MODEL), rows),
            pl.BlockSpec((tm, D_MODEL), rows),
            pl.BlockSpec((tm, D_MODEL), rows),
            pl.BlockSpec((tm, D_MODEL), lambda i: (i, P_GA)),
            pl.BlockSpec((tm, D_MODEL), lambda i: (i, P_GB)),
            pl.BlockSpec((D_MODEL, D_MODEL), whole),
            pl.BlockSpec((D_MODEL, D_MODEL), whole),
            pl.BlockSpec((D_MODEL, D_MODEL), whole),
        ],
        out_specs=pl.BlockSpec((tm, D_MODEL), rows),
        out_shape=jax.ShapeDtypeStruct((t, D_MODEL), f32),
        compiler_params=pltpu.CompilerParams(
            dimension_semantics=("arbitrary",), vmem_limit_bytes=VMEM_LIMIT),
        name="merge",
    )(x2, o_hg, o_sb, proj, proj, w_hg, w_sb, w_o)


def _ffn_kernel(x_ref, gain_ref, wi_ref, wd_ref, o_ref, act_scr):
    x = x_ref[...]
    ms = jnp.mean(x * x, axis=-1, keepdims=True)
    h = (x * lax.rsqrt(ms + EPS) * gain_ref[...]).astype(bf16)
    for j in range(FF_HIDDEN // FF_CHUNK):
        cols = slice(j * FF_CHUNK, (j + 1) * FF_CHUNK)
        up_cols = slice(FF_HIDDEN + j * FF_CHUNK, FF_HIDDEN + (j + 1) * FF_CHUNK)
        gate = _dot(h, wi_ref[:, cols])
        up = _dot(h, wi_ref[:, up_cols])
        act_scr[:, cols] = (gate * _sigmoid(gate) * up).astype(bf16)
    o_ref[...] = x + _dot(act_scr[...], wd_ref[...])


def _ffn(x1, gain, w_in_bf, w_out_bf, tm=1024):
    t = x1.shape[0]
    resident = dict(pipeline_mode=pl.Buffered(1))
    return pl.pallas_call(
        _ffn_kernel,
        grid=(t // tm,),
        in_specs=[
            pl.BlockSpec((tm, D_MODEL), lambda i: (i, 0)),
            pl.BlockSpec((1, D_MODEL), lambda i: (0, 0)),
            pl.BlockSpec((D_MODEL, 2 * FF_HIDDEN), lambda i: (0, 0), **resident),
            pl.BlockSpec((FF_HIDDEN, D_MODEL), lambda i: (0, 0), **resident),
        ],
        out_specs=pl.BlockSpec((tm, D_MODEL), lambda i: (i, 0)),
        out_shape=jax.ShapeDtypeStruct((t, D_MODEL), f32),
        scratch_shapes=[pltpu.VMEM((tm, FF_HIDDEN), bf16)],
        compiler_params=pltpu.CompilerParams(
            dimension_semantics=("arbitrary",), vmem_limit_bytes=VMEM_LIMIT),
        name="ffn",
    )(x1, gain, w_in_bf, w_out_bf)


def kernel(x, norm1_gain, w_in, lb_logits, hg_out_norm, sb_q_norm, sb_k_norm,
           w_hg_out, w_sb_out, w_o, norm2_gain, w_ffn_in, w_ffn_out):
    batch, seq, d = x.shape
    depth = norm1_gain.shape[0]
    x2 = x.reshape(batch * seq, d)
    for layer in range(depth):
        gain1 = norm1_gain[layer][None, :]
        proj = _in_proj(x2, gain1, w_in[layer])
        log_f, one_minus_f = _forget_proj(x2, gain1, w_in[layer], lb_logits, layer)
        o_hg = _hgrn2(proj, log_f, one_minus_f, hg_out_norm[layer], batch, seq)
        o_sb = _sb_attn(proj, sb_q_norm[layer], sb_k_norm[layer], batch, seq)
        x1 = _merge(x2, o_hg, o_sb, proj, w_hg_out[layer].astype(bf16),
                    w_sb_out[layer].astype(bf16), w_o[layer].astype(bf16))
        x2 = _ffn(x1, norm2_gain[layer][None, :], w_ffn_in[layer].astype(bf16),
                  w_ffn_out[layer].astype(bf16))
    return x2.reshape(batch, seq, d)
```

```python
import functools
import math

import jax
import jax.numpy as jnp
from jax import lax
from jax.experimental import pallas as pl
from jax.experimental.pallas import tpu as pltpu

D_MODEL = 1024
HG_HEADS = 8
HG_DK = 128
HG_DV = 128
SB_HEADS = 16
SB_DH = 64
FF_HIDDEN = 2816
FF_CHUNK = 256
EPS = 1e-6
G_HQ, G_HF, G_HI, G_HG, G_SQ, G_SK, G_SV, G_GA, G_GB = range(9)
P_SOURCE = (G_HQ, G_HI, G_HG, G_SQ, G_SK, G_SV, G_GA, G_GB)
P_HQ, P_HI, P_HG, P_SQ, P_SK, P_SV, P_GA, P_GB = range(8)
P_H = 8

LANES = 128
HG_CHUNK = 64
HG_PAR = 4
HG_ROWS = 1024
HG_EXP_CLAMP = 80.0
SB_T = 128
SB_QROWS = 1024
SB_DONE_LOG2 = 128.0
VMEM_LIMIT = 56 * 1024 * 1024

f32 = jnp.float32
bf16 = jnp.bfloat16
NEG_INF = float("-inf")


def _sigmoid(x):
    return 1.0 / (1.0 + jnp.exp(-x))


def _sigmoid_tanh(x):
    return 0.5 * jnp.tanh(0.5 * x) + 0.5


def _dot(a, b):
    return jnp.dot(a, b, preferred_element_type=f32)


def _dot_nt(a, b):
    return lax.dot_general(a, b, (((1,), (1,)), ((), ())), preferred_element_type=f32)


def _dot_tn(a, b):
    return lax.dot_general(a, b, (((0,), (0,)), ((), ())), preferred_element_type=f32)


def _split2(x):
    hi = x.astype(bf16)
    lo = (x - hi.astype(f32)).astype(bf16)
    return hi, lo


def _rms_norm_bf16(x, gain):
    ms = jnp.mean(x * x, axis=-1, keepdims=True)
    return (x * lax.rsqrt(ms + EPS) * gain).astype(bf16)


def _inproj_kernel(x_ref, gain_ref, w_ref, o_ref, h_scr):
    j = pl.program_id(1)

    @pl.when(j == 0)
    def _():
        h_scr[...] = _rms_norm_bf16(x_ref[...], gain_ref[...])

    is_gate = (j == P_HG) | (j == P_GA) | (j == P_GB)

    @pl.when(is_gate)
    def _():
        o_ref[...] = _sigmoid_tanh(_dot(h_scr[...], w_ref[...].astype(bf16))).astype(o_ref.dtype)

    @pl.when(jnp.logical_not(is_gate) & (j != P_H))
    def _():
        o_ref[...] = _dot(h_scr[...], w_ref[...].astype(bf16)).astype(o_ref.dtype)

    @pl.when(j == P_H)
    def _():
        o_ref[...] = h_scr[...]


def _in_proj(x2, gain, w_in, tm=2048):
    t = x2.shape[0]
    n_groups = P_H + 1
    last_w = w_in.shape[1] // D_MODEL - 1
    return pl.pallas_call(
        _inproj_kernel,
        grid=(t // tm, n_groups),
        in_specs=[
            pl.BlockSpec((tm, D_MODEL), lambda i, j: (i, 0)),
            pl.BlockSpec((1, D_MODEL), lambda i, j: (0, 0)),
            pl.BlockSpec((D_MODEL, D_MODEL),
                         lambda i, j: (0, jnp.minimum(jnp.where(j >= G_HF, j + 1, j), last_w))),
        ],
        out_specs=pl.BlockSpec((tm, D_MODEL), lambda i, j: (i, j)),
        out_shape=jax.ShapeDtypeStruct((t, n_groups * D_MODEL), bf16),
        scratch_shapes=[pltpu.VMEM((tm, D_MODEL), bf16)],
        compiler_params=pltpu.CompilerParams(
            dimension_semantics=("arbitrary", "arbitrary"), vmem_limit_bytes=VMEM_LIMIT),
        name="in_proj",
    )(x2, gain, w_in)


def _forget_kernel(h_ref, w_ref, lbl_ref, ghi_ref, glo_ref, k_ref, *, layer):
    l = lbl_ref[...]
    e = jnp.exp(l - jnp.max(l, axis=0, keepdims=True))
    lb = jnp.sum(e[0:layer + 1, :], axis=0, keepdims=True) / jnp.sum(e, axis=0, keepdims=True)
    f_mid = 0.5 * (1.0 + lb)
    f_amp = 0.5 * (1.0 - lb)
    w = w_ref[...].astype(bf16)
    n_blk = 4
    blk = h_ref.shape[0] // n_blk
    ys = [_dot(h_ref[b * blk:(b + 1) * blk, :], w) for b in range(n_blk)]
    for b in range(n_blk):
        rows = slice(b * blk, (b + 1) * blk)
        f = f_mid + f_amp * jnp.tanh(0.5 * ys[b])
        ghi_ref[rows, :], glo_ref[rows, :] = _split2(jnp.log(f))
        k_ref[rows, :] = (1.0 - f).astype(k_ref.dtype)


def _forget_proj(proj, w_in, lb_logits, layer, tm=1024):
    t = proj.shape[0]
    rows = lambda i: (i, 0)
    out = jax.ShapeDtypeStruct((t, D_MODEL), bf16)
    return pl.pallas_call(
        functools.partial(_forget_kernel, layer=layer),
        grid=(t // tm,),
        in_specs=[
            pl.BlockSpec((tm, D_MODEL), lambda i: (i, P_H)),
            pl.BlockSpec((D_MODEL, D_MODEL), lambda i: (0, G_HF)),
            pl.BlockSpec(lb_logits.shape, lambda i: (0, 0)),
        ],
        out_specs=[pl.BlockSpec((tm, D_MODEL), rows)] * 3,
        out_shape=[out] * 3,
        compiler_params=pltpu.CompilerParams(
            dimension_semantics=("arbitrary",), vmem_limit_bytes=VMEM_LIMIT),
        name="forget_proj",
    )(proj, w_in, lb_logits)


def _hgrn_kernel(q_ref, ghi_ref, glo_ref, k_ref, i_ref, og_ref, gain_ref, o_ref, st_scr):
    c_len = HG_CHUNK
    n_blk = q_ref.shape[0] // (HG_PAR * c_len)

    @pl.when(pl.program_id(1) == 0)
    def _():
        st_scr[...] = jnp.zeros_like(st_scr)

    gain = gain_ref[...]

    row = lax.broadcasted_iota(jnp.int32, (c_len, c_len), 0)
    col = lax.broadcasted_iota(jnp.int32, (c_len, c_len), 1)
    causal = row >= col
    tri = jnp.where(causal, 1.0, 0.0).astype(bf16)
    tri2 = jnp.concatenate([tri, tri], axis=1)
    heads = range(HG_HEADS)
    sl = [slice(h * HG_DK, (h + 1) * HG_DK) for h in heads]

    def chunk_operands(c):
        rows = pl.ds(pl.multiple_of(c * c_len, c_len), c_len)
        k = k_ref[rows, :].astype(f32)
        cum = _dot(tri2, jnp.concatenate([ghi_ref[rows, :], glo_ref[rows, :]], axis=0))
        last = cum[c_len - 1:c_len, :]
        mid = cum[c_len // 2 - 1:c_len // 2, :]
        q = q_ref[rows, :].astype(f32)
        return dict(
            rows=rows,
            qt=(q * jnp.exp(jnp.minimum(cum - mid, HG_EXP_CLAMP))).astype(bf16),
            kt=(k * jnp.exp(jnp.minimum(mid - cum, HG_EXP_CLAMP))).astype(bf16),
            qd=(q * jnp.exp(cum)).astype(bf16),
            kd=(k * jnp.exp(last - cum)).astype(bf16),
            e_last=jnp.exp(last),
            v=i_ref[rows, :],
            out_scale=gain * og_ref[rows, :].astype(f32),
        )

    def body(it, carry):
        cs = [chunk_operands(it * HG_PAR + p) for p in range(HG_PAR)]
        scores =[[jnp.where(causal, _dot_nt(c["qt"][:, sl[h]], c["kt"][:, sl[h]]), 0.0).astype(bf16)
                   for h in heads] for c in cs]
        upd = [[_dot_tn(c["v"][:, sl[h]], c["kd"][:, sl[h]]) for h in heads] for c in cs]
        states = [st_scr[h] for h in heads]
        inter = []
        for p, c in enumerate(cs):
            inter.append([_dot_nt(c["qd"][:, sl[h]], states[h].astype(bf16)) for h in heads])
            states = [states[h] * c["e_last"][:, sl[h]] + upd[p][h] for h in heads]
        for h in heads:
            st_scr[h] = states[h]
        for p, c in enumerate(cs):
            outs = []
            for h in heads:
                o = _dot(scores[p][h], c["v"][:, sl[h]]) + inter[p][h]
                ms = jnp.mean(o * o, axis=-1, keepdims=True)
                outs.append((o * lax.rsqrt(ms + EPS) * c["out_scale"][:, sl[h]]).astype(o_ref.dtype))
            o_ref[c["rows"], :] = jnp.concatenate(outs, axis=1)
        return carry

    lax.fori_loop(0, n_blk, body, 0)


def _hgrn2(proj, g_hi, g_lo, one_minus_f, hg_gain, batch, seq):
    t = proj.shape[0]
    n_rt = seq // HG_ROWS

    def grp(group):
        return pl.BlockSpec((HG_ROWS, D_MODEL), lambda b, r: (b * n_rt + r, group))

    return pl.pallas_call(
        _hgrn_kernel,
        grid=(batch, n_rt),
        in_specs=[
            grp(P_HQ), grp(0), grp(0), grp(0), grp(P_HI), grp(P_HG),
            pl.BlockSpec((1, D_MODEL), lambda b, r: (0, 0)),
        ],
        out_specs=grp(0),
        out_shape=jax.ShapeDtypeStruct((t, HG_HEADS * HG_DV), bf16),
        scratch_shapes=[pltpu.VMEM((HG_HEADS, HG_DV, HG_DK), f32)],
        compiler_params=pltpu.CompilerParams(
            dimension_semantics=("arbitrary", "arbitrary"), vmem_limit_bytes=VMEM_LIMIT),
        name="hgrn2",
    )(proj, g_hi, g_lo, one_minus_f, proj, proj, hg_gain.reshape(1, HG_HEADS * HG_DV))


def _pair_rms(blocks, gain, group_mean):
    ms = [_dot((x * x).astype(bf16), group_mean) for x in blocks]
    return [x * lax.rsqrt(m + EPS) * gain for x, m in zip(blocks, ms)]


def _softplus2(z):
    return jnp.maximum(z, 0.0) + jnp.log2(1.0 + jnp.exp2(-jnp.abs(z)))


def _hi_lo(sp):
    return jnp.concatenate(_split2(sp), axis=1)


def _sb_kernel(q_ref, k_ref, v_ref, qg_ref, kg_ref, o_ref, kn_scr, q2_scr, acc_scr, car_scr):
    step = pl.program_id(2)
    seq = k_ref.shape[0]
    n_sub = SB_QROWS // SB_T
    lane = lax.broadcasted_iota(jnp.int32, (1, LANES), 1)
    lo_mask = lane < SB_DH

    gr = lax.broadcasted_iota(jnp.int32, (LANES, LANES), 0) // SB_DH
    gc = lax.broadcasted_iota(jnp.int32, (LANES, LANES), 1) // SB_DH
    group_mean = jnp.where(gr == gc, 1.0 / SB_DH, 0.0).astype(bf16)

    @pl.when(step == 0)
    def _():
        prep = 512

        blocks = [k_ref[r * prep:(r + 1) * prep, :].astype(f32) for r in range(seq // prep)]
        for r, kn in enumerate(_pair_rms(blocks, kg_ref[...], group_mean)):
            kn_scr[r * prep:(r + 1) * prep, :] = kn.astype(bf16)

    q_scale = math.log2(math.e) / math.sqrt(SB_DH)
    qn = _pair_rms([q_ref[c * SB_T:(c + 1) * SB_T, :].astype(f32) for c in range(n_sub)],
                   qg_ref[...] * q_scale, group_mean)
    q2 = [jnp.concatenate([jnp.where(lo_mask, q, 0.0).astype(bf16),
                           jnp.where(lo_mask, 0.0, q).astype(bf16)], axis=0) for q in qn]

    row = lax.broadcasted_iota(jnp.int32, (2 * SB_T, SB_T), 0) % SB_T
    col = lax.broadcasted_iota(jnp.int32, (2 * SB_T, SB_T), 1)
    earlier = col < row
    urow = lax.broadcasted_iota(jnp.int32, (2 * SB_T, 2 * SB_T), 0) % SB_T
    ucol = lax.broadcasted_iota(jnp.int32, (2 * SB_T, 2 * SB_T), 1)
    suffix2 = jnp.where((urow >= ucol) | (ucol >= SB_T), 1.0, 0.0).astype(bf16)

    first = step == 0
    k_start = [jnp.where(first, 0, step * n_sub - 1) * SB_T] + [
        (step * n_sub + c - 1) * SB_T for c in range(1, n_sub)]
    pair_keys = [pl.ds(pl.multiple_of(k0, SB_T), 2 * SB_T) for k0 in k_start]

    zs = [_dot_nt(q2[c], kn_scr[pair_keys[c], :]) for c in range(n_sub)]
    off = jnp.where(first, 0, SB_T)
    zl = [zs[c][:, :SB_T] for c in range(n_sub)]
    zd = [jnp.where(col + SB_T < row + off if c == 0 else earlier, zs[c][:, SB_T:], NEG_INF)
          for c in range(n_sub)]
    zl[0] = jnp.where(col < row + off, zl[0], NEG_INF)
    fulls = [_dot(jnp.concatenate([_hi_lo(_softplus2(zd[c])), _hi_lo(_softplus2(zl[c]))], axis=0), suffix2)
             for c in range(n_sub)]
    def still_live(car, next_tile):
        m = jnp.min(jnp.min(car, axis=1, keepdims=True), axis=0, keepdims=True)
        return ((next_tile >= 0) & (m[0, 0] <= SB_DONE_LOG2)).astype(jnp.int32)

    accs, cars, lives = [], [], []
    for c in range(n_sub):
        full = fulls[c]
        rd, car = full[:2 * SB_T, :SB_T], full[:2 * SB_T, SB_T:]
        rl = full[2 * SB_T:, :SB_T] + car
        car = car + full[2 * SB_T:, SB_T:]
        a = jnp.concatenate([jnp.exp2(zl[c] - rl), jnp.exp2(zd[c] - rd)], axis=1)
        accs.append(_dot(a.astype(bf16), v_ref[pair_keys[c], :]))
        cars.append(car)
        lives.append(still_live(car, step * n_sub + c - 2))
    for c in range(n_sub):
        q2_scr[c] = q2[c]
        acc_scr[c] = accs[c]
        car_scr[c] = cars[c]

    def one_more_tile(c, d):
        kb = step * n_sub + c - d
        keys = pl.ds(pl.multiple_of(kb * SB_T, SB_T), SB_T)
        z = _dot_nt(q2_scr[c], kn_scr[keys, :])
        full = _dot(_hi_lo(_softplus2(z)), suffix2)
        car = car_scr[c]
        a = jnp.exp2(z - (full[:, :SB_T] + car))
        acc_scr[c] += _dot(a.astype(bf16), v_ref[keys, :])
        car = car + full[:, SB_T:]
        car_scr[c] = car
        return still_live(car, kb - 1)

    def more(state):
        live = state[1]
        for flag in state[2:]:
            live = live | flag
        return live > 0

    def walk(state):
        d = state[0]
        flags = [lax.cond(state[1 + c] > 0, functools.partial(one_more_tile, c, d), lambda: jnp.int32(0))
                 for c in range(n_sub)]
        return (d + 1, *flags)

    lax.while_loop(more, walk, (jnp.int32(2), *lives))

    for c in range(n_sub):
        o_ref[c * SB_T:(c + 1) * SB_T, :] = jnp.where(
            lo_mask, acc_scr[c, :SB_T, :], acc_scr[c, SB_T:, :]).astype(o_ref.dtype)


def _sb_attn(proj, q_gain, k_gain, batch, seq):
    t = proj.shape[0]
    hpg = D_MODEL // LANES
    n_pairs = SB_HEADS // 2
    n_qt = seq // SB_QROWS
    n_sub = SB_QROWS // SB_T
    return pl.pallas_call(
        _sb_kernel,
        grid=(batch, n_pairs, n_qt),
        in_specs=[
            pl.BlockSpec((SB_QROWS, LANES), lambda b, p, i: (b * n_qt + i, P_SQ * hpg + p)),
            pl.BlockSpec((seq, LANES), lambda b, p, i: (b, P_SK * hpg + p)),
            pl.BlockSpec((seq, LANES), lambda b, p, i: (b, P_SV * hpg + p)),
            pl.BlockSpec((None, 1, LANES), lambda b, p, i: (p, 0, 0)),
            pl.BlockSpec((None, 1, LANES), lambda b, p, i: (p, 0, 0)),
        ],
        out_specs=pl.BlockSpec((SB_QROWS, LANES), lambda b, p, i: (b * n_qt + i, p)),
        out_shape=jax.ShapeDtypeStruct((t, SB_HEADS * SB_DH), bf16),
        scratch_shapes=[
            pltpu.VMEM((seq, LANES), bf16),
            pltpu.VMEM((n_sub, 2 * SB_T, LANES), bf16),
            pltpu.VMEM((n_sub, 2 * SB_T, LANES), f32),
            pltpu.VMEM((n_sub, 2 * SB_T, LANES), f32),
        ],
        compiler_params=pltpu.CompilerParams(
            dimension_semantics=("arbitrary", "arbitrary", "arbitrary"), vmem_limit_bytes=VMEM_LIMIT),
        name="sb_attn",
    )(proj, proj, proj, q_gain.reshape(n_pairs, 1, LANES), k_gain.reshape(n_pairs, 1, LANES))


def _merge_kernel(x_ref, ohg_ref, osb_ref, ga_ref, gb_ref, whg_ref, wsb_ref, wo_ref, o_ref):
    half = x_ref.shape[0] // 2
    parts = [slice(0, half), slice(half, 2 * half)]
    y_hg = [_dot(ohg_ref[r, :], whg_ref[...]) for r in parts]
    y_sb = [_dot(osb_ref[r, :], wsb_ref[...]) for r in parts]
    for r, yh, ys in zip(parts, y_hg, y_sb):
        mixed = ga_ref[r, :].astype(f32) * yh + gb_ref[r, :].astype(f32) * ys
        o_ref[r, :] = x_ref[r, :] + _dot(mixed.astype(bf16), wo_ref[...])


def _merge(x2, o_hg, o_sb, proj, w_hg, w_sb, w_o, tm=1024):
    t = x2.shape[0]
    rows = lambda i: (i, 0)
    resident = dict(pipeline_mode=pl.Buffered(1))
    return pl.pallas_call(
        _merge_kernel,
        grid=(t // tm,),
        in_specs=[
            pl.BlockSpec((tm, D_MODEL), rows),
            pl.BlockSpec((tm, D_MODEL), rows),
            pl.BlockSpec((tm, D_MODEL), rows),
            pl.BlockSpec((tm, D_MODEL), lambda i: (i, P_GA)),
            pl.BlockSpec((tm, D_MODEL), lambda i: (i, P_GB)),
            pl.BlockSpec((D_MODEL, D_MODEL), lambda i: (0, 0), **resident),
            pl.BlockSpec((D_MODEL, D_MODEL), lambda i: (0, 0), **resident),
            pl.BlockSpec((D_MODEL, D_MODEL), lambda i: (0, 0), **resident),
        ],
        out_specs=pl.BlockSpec((tm, D_MODEL), rows),
        out_shape=jax.ShapeDtypeStruct((t, D_MODEL), f32),
        compiler_params=pltpu.CompilerParams(
            dimension_semantics=("arbitrary",), vmem_limit_bytes=VMEM_LIMIT),
        name="merge",
    )(x2, o_hg, o_sb, proj, proj, w_hg, w_sb, w_o)


def _ffn_kernel(x_ref, gain_ref, wi_ref, wd_ref, o_ref, act_scr):
    x = x_ref[...]
    ms = jnp.mean(x * x, axis=-1, keepdims=True)
    h = (x * lax.rsqrt(ms + EPS) * gain_ref[...]).astype(bf16)
    for j in range(FF_HIDDEN // FF_CHUNK):
        cols = slice(j * FF_CHUNK, (j + 1) * FF_CHUNK)
        up_cols = slice(FF_HIDDEN + j * FF_CHUNK, FF_HIDDEN + (j + 1) * FF_CHUNK)
        gate = _dot(h, wi_ref[:, cols])
        up = _dot(h, wi_ref[:, up_cols])
        act_scr[:, cols] = (gate * _sigmoid(gate) * up).astype(bf16)
    o_ref[...] = x + _dot(act_scr[...], wd_ref[...])


def _ffn(x1, gain, w_in_bf, w_out_bf, tm=1024):
    t = x1.shape[0]
    resident = dict(pipeline_mode=pl.Buffered(1))
    return pl.pallas_call(
        _ffn_kernel,
        grid=(t // tm,),
        in_specs=[
            pl.BlockSpec((tm, D_MODEL), lambda i: (i, 0)),
            pl.BlockSpec((1, D_MODEL), lambda i: (0, 0)),
            pl.BlockSpec((D_MODEL, 2 * FF_HIDDEN), lambda i: (0, 0), **resident),
            pl.BlockSpec((FF_HIDDEN, D_MODEL), lambda i: (0, 0), **resident),
        ],
        out_specs=pl.BlockSpec((tm, D_MODEL), lambda i: (i, 0)),
        out_shape=jax.ShapeDtypeStruct((t, D_MODEL), f32),
        scratch_shapes=[pltpu.VMEM((tm, FF_HIDDEN), bf16)],
        compiler_params=pltpu.CompilerParams(
            dimension_semantics=("arbitrary",), vmem_limit_bytes=VMEM_LIMIT),
        name="ffn",
    )(x1, gain, w_in_bf, w_out_bf)


def kernel(x, norm1_gain, w_in, lb_logits, hg_out_norm, sb_q_norm, sb_k_norm,
           w_hg_out, w_sb_out, w_o, norm2_gain, w_ffn_in, w_ffn_out):
    batch, seq, d = x.shape
    depth = norm1_gain.shape[0]
    x2 = x.reshape(batch * seq, d)
    for layer in range(depth):
        gain1 = norm1_gain[layer][None, :]
        proj = _in_proj(x2, gain1, w_in[layer])
        g_hi, g_lo, one_minus_f = _forget_proj(proj, w_in[layer], lb_logits, layer)
        o_hg = _hgrn2(proj, g_hi, g_lo, one_minus_f, hg_out_norm[layer], batch, seq)
        o_sb = _sb_attn(proj, sb_q_norm[layer], sb_k_norm[layer], batch, seq)
        x1 = _merge(x2, o_hg, o_sb, proj, w_hg_out[layer].astype(bf16),
                    w_sb_out[layer].astype(bf16), w_o[layer].astype(bf16))
        x2 = _ffn(x1, norm2_gain[layer][None, :], w_ffn_in[layer].astype(bf16),
                  w_ffn_out[layer].astype(bf16))
    return x2.reshape(batch, seq, d)
```

```python
import functools
import math

import jax
import jax.numpy as jnp
from jax import lax
from jax.experimental import pallas as pl
from jax.experimental.pallas import tpu as pltpu

D_MODEL = 1024
HG_HEADS = 8
HG_DK = 128
HG_DV = 128
SB_HEADS = 16
SB_DH = 64
FF_HIDDEN = 2816
FF_CHUNK = 256
EPS = 1e-6
G_HQ, G_HF, G_HI, G_HG, G_SQ, G_SK, G_SV, G_GA, G_GB = range(9)
P_SOURCE = (G_HQ, G_HI, G_HG, G_SQ, G_SK, G_SV, G_GA, G_GB)
P_HQ, P_HI, P_HG, P_SQ, P_SK, P_SV, P_GA, P_GB = range(8)

LANES = 128
HG_CHUNK = 64
HG_PAR = 4
HG_ROWS = 1024
HG_EXP_CLAMP = 80.0
SB_T = 128
SB_QROWS = 1024
SB_DONE_LOG2 = 128.0
VMEM_LIMIT = 56 * 1024 * 1024

f32 = jnp.float32
bf16 = jnp.bfloat16
NEG_INF = float("-inf")


def _sigmoid(x):
    return 1.0 / (1.0 + jnp.exp(-x))


def _sigmoid_tanh(x):
    return 0.5 * jnp.tanh(0.5 * x) + 0.5


def _dot(a, b):
    return jnp.dot(a, b, preferred_element_type=f32)


def _dot_nt(a, b):
    return lax.dot_general(a, b, (((1,), (1,)), ((), ())), preferred_element_type=f32)


def _dot_tn(a, b):
    return lax.dot_general(a, b, (((0,), (0,)), ((), ())), preferred_element_type=f32)


def _split2(x):
    hi = x.astype(bf16)
    lo = (x - hi.astype(f32)).astype(bf16)
    return hi, lo


def _rms_norm_bf16(x, gain):
    ms = jnp.mean(x * x, axis=-1, keepdims=True)
    return (x * lax.rsqrt(ms + EPS) * gain).astype(bf16)


def _inproj_kernel(x_ref, gain_ref, w_ref, o_ref, h_ref):
    j = pl.program_id(1)

    @pl.when(j == 0)
    def _():
        h_ref[...] = _rms_norm_bf16(x_ref[...], gain_ref[...])

    is_gate = (j == P_HG) | (j == P_GA) | (j == P_GB)

    @pl.when(is_gate)
    def _():
        o_ref[...] = _sigmoid_tanh(_dot(h_ref[...], w_ref[...].astype(bf16))).astype(o_ref.dtype)

    @pl.when(jnp.logical_not(is_gate))
    def _():
        o_ref[...] = _dot(h_ref[...], w_ref[...].astype(bf16)).astype(o_ref.dtype)


def _in_proj(x2, gain, w_in, tm=2048):
    t = x2.shape[0]
    n_groups = len(P_SOURCE)
    return pl.pallas_call(
        _inproj_kernel,
        grid=(t // tm, n_groups),
        in_specs=[
            pl.BlockSpec((tm, D_MODEL), lambda i, j: (i, 0)),
            pl.BlockSpec((1, D_MODEL), lambda i, j: (0, 0)),
            pl.BlockSpec((D_MODEL, D_MODEL), lambda i, j: (0, jnp.where(j >= G_HF, j + 1, j))),
        ],
        out_specs=[
            pl.BlockSpec((tm, D_MODEL), lambda i, j: (i, j)),
            pl.BlockSpec((tm, D_MODEL), lambda i, j: (i, 0)),
        ],
        out_shape=[
            jax.ShapeDtypeStruct((t, n_groups * D_MODEL), bf16),
            jax.ShapeDtypeStruct((t, D_MODEL), bf16),
        ],
        compiler_params=pltpu.CompilerParams(
            dimension_semantics=("arbitrary", "arbitrary"), vmem_limit_bytes=VMEM_LIMIT),
        name="in_proj",
    )(x2, gain, w_in)


def _forget_kernel(h_ref, w_ref, lbl_ref, ghi_ref, glo_ref, *, layer):
    l = lbl_ref[...]
    e = jnp.exp(l - jnp.max(l, axis=0, keepdims=True))
    lb = jnp.sum(e[0:layer + 1, :], axis=0, keepdims=True) / jnp.sum(e, axis=0, keepdims=True)
    f_mid = 0.5 * (1.0 + lb)
    f_amp = 0.5 * (1.0 - lb)
    w = w_ref[...].astype(bf16)
    n_blk = 4
    blk = h_ref.shape[0] // n_blk
    ys = [_dot(h_ref[b * blk:(b + 1) * blk, :], w) for b in range(n_blk)]
    for b in range(n_blk):
        rows = slice(b * blk, (b + 1) * blk)
        f = f_mid + f_amp * jnp.tanh(0.5 * ys[b])
        ghi_ref[rows, :], glo_ref[rows, :] = _split2(jnp.log(f))


def _forget_proj(h, w_in, lb_logits, layer, tm=1024):
    t = h.shape[0]
    rows = lambda i: (i, 0)
    out = jax.ShapeDtypeStruct((t, D_MODEL), bf16)
    return pl.pallas_call(
        functools.partial(_forget_kernel, layer=layer),
        grid=(t // tm,),
        in_specs=[
            pl.BlockSpec((tm, D_MODEL), rows),
            pl.BlockSpec((D_MODEL, D_MODEL), lambda i: (0, G_HF)),
            pl.BlockSpec(lb_logits.shape, lambda i: (0, 0)),
        ],
        out_specs=[pl.BlockSpec((tm, D_MODEL), rows)] * 2,
        out_shape=[out] * 2,
        compiler_params=pltpu.CompilerParams(
            dimension_semantics=("arbitrary",), vmem_limit_bytes=VMEM_LIMIT),
        name="forget_proj",
    )(h, w_in, lb_logits)


def _hgrn_kernel(q_ref, ghi_ref, glo_ref, i_ref, og_ref, gain_ref, o_ref, st_scr):
    c_len = HG_CHUNK
    n_blk = q_ref.shape[0] // (HG_PAR * c_len)

    @pl.when(pl.program_id(1) == 0)
    def _():
        st_scr[...] = jnp.zeros_like(st_scr)

    gain = gain_ref[...]

    row = lax.broadcasted_iota(jnp.int32, (c_len, c_len), 0)
    col = lax.broadcasted_iota(jnp.int32, (c_len, c_len), 1)
    causal = row >= col
    tri = jnp.where(causal, 1.0, 0.0).astype(bf16)
    tri2 = jnp.concatenate([tri, tri], axis=1)
    heads = range(HG_HEADS)
    sl = [slice(h * HG_DK, (h + 1) * HG_DK) for h in heads]

    def chunk_operands(c):
        rows = pl.ds(pl.multiple_of(c * c_len, c_len), c_len)
        g_hi, g_lo = ghi_ref[rows, :], glo_ref[rows, :]
        cum = _dot(tri2, jnp.concatenate([g_hi, g_lo], axis=0))
        k = 1.0 - jnp.exp(g_hi.astype(f32) + g_lo.astype(f32))
        last = cum[c_len - 1:c_len, :]
        mid = cum[c_len // 2 - 1:c_len // 2, :]
        q = q_ref[rows, :].astype(f32)
        return dict(
            rows=rows,
            qt=(q * jnp.exp(jnp.minimum(cum - mid, HG_EXP_CLAMP))).astype(bf16),
            kt=(k * jnp.exp(jnp.minimum(mid - cum, HG_EXP_CLAMP))).astype(bf16),
            qd=(q * jnp.exp(cum)).astype(bf16),
            kd=(k * jnp.exp(last - cum)).astype(bf16),
            e_last=jnp.exp(last),
            v=i_ref[rows, :],
            out_scale=gain * og_ref[rows, :].astype(f32),
        )

    def body(it, carry):
        cs = [chunk_operands(it * HG_PAR + p) for p in range(HG_PAR)]
        scores =[[jnp.where(causal, _dot_nt(c["qt"][:, sl[h]], c["kt"][:, sl[h]]), 0.0).astype(bf16)
                   for h in heads] for c in cs]
        upd = [[_dot_tn(c["v"][:, sl[h]], c["kd"][:, sl[h]]) for h in heads] for c in cs]
        states = [st_scr[h] for h in heads]
        inter = []
        for p, c in enumerate(cs):
            inter.append([_dot_nt(c["qd"][:, sl[h]], states[h].astype(bf16)) for h in heads])
            states = [states[h] * c["e_last"][:, sl[h]] + upd[p][h] for h in heads]
        for h in heads:
            st_scr[h] = states[h]
        for p, c in enumerate(cs):
            outs = []
            for h in heads:
                o = _dot(scores[p][h], c["v"][:, sl[h]]) + inter[p][h]
                ms = jnp.mean(o * o, axis=-1, keepdims=True)
                outs.append((o * lax.rsqrt(ms + EPS) * c["out_scale"][:, sl[h]]).astype(o_ref.dtype))
            o_ref[c["rows"], :] = jnp.concatenate(outs, axis=1)
        return carry

    lax.fori_loop(0, n_blk, body, 0)


def _hgrn2(proj, g_hi, g_lo, hg_gain, batch, seq):
    t = proj.shape[0]
    n_rt = seq // HG_ROWS

    def grp(group):
        return pl.BlockSpec((HG_ROWS, D_MODEL), lambda b, r: (b * n_rt + r, group))

    return pl.pallas_call(
        _hgrn_kernel,
        grid=(batch, n_rt),
        in_specs=[
            grp(P_HQ), grp(0), grp(0), grp(P_HI), grp(P_HG),
            pl.BlockSpec((1, D_MODEL), lambda b, r: (0, 0)),
        ],
        out_specs=grp(0),
        out_shape=jax.ShapeDtypeStruct((t, HG_HEADS * HG_DV), bf16),
        scratch_shapes=[pltpu.VMEM((HG_HEADS, HG_DV, HG_DK), f32)],
        compiler_params=pltpu.CompilerParams(
            dimension_semantics=("arbitrary", "arbitrary"), vmem_limit_bytes=VMEM_LIMIT),
        name="hgrn2",
    )(proj, g_hi, g_lo, proj, proj, hg_gain.reshape(1, HG_HEADS * HG_DV))


def _pair_rms(blocks, gain, group_mean):
    ms = [_dot((x * x).astype(bf16), group_mean) for x in blocks]
    return [x * lax.rsqrt(m + EPS) * gain for x, m in zip(blocks, ms)]


def _softplus2(z):
    return jnp.maximum(z, 0.0) + jnp.log2(1.0 + jnp.exp2(-jnp.abs(z)))


def _sb_kernel(q_ref, k_ref, v_ref, qg_ref, kg_ref, o_ref, kn_scr, q2_scr, acc_scr, car_scr):
    step = pl.program_id(2)
    seq = k_ref.shape[0]
    n_sub = SB_QROWS // SB_T
    lane = lax.broadcasted_iota(jnp.int32, (1, LANES), 1)
    lo_mask = lane < SB_DH

    gr = lax.broadcasted_iota(jnp.int32, (LANES, LANES), 0) // SB_DH
    gc = lax.broadcasted_iota(jnp.int32, (LANES, LANES), 1) // SB_DH
    group_mean = jnp.where(gr == gc, 1.0 / SB_DH, 0.0).astype(bf16)

    @pl.when(step == 0)
    def _():
        prep = 512

        blocks = [k_ref[r * prep:(r + 1) * prep, :].astype(f32) for r in range(seq // prep)]
        for r, kn in enumerate(_pair_rms(blocks, kg_ref[...], group_mean)):
            kn_scr[r * prep:(r + 1) * prep, :] = kn.astype(bf16)

    q_scale = math.log2(math.e) / math.sqrt(SB_DH)
    qn = _pair_rms([q_ref[c * SB_T:(c + 1) * SB_T, :].astype(f32) for c in range(n_sub)],
                   qg_ref[...] * q_scale, group_mean)
    q2 = [jnp.concatenate([jnp.where(lo_mask, q, 0.0).astype(bf16),
                           jnp.where(lo_mask, 0.0, q).astype(bf16)], axis=0) for q in qn]

    row = lax.broadcasted_iota(jnp.int32, (2 * SB_T, SB_T), 0) % SB_T
    col = lax.broadcasted_iota(jnp.int32, (2 * SB_T, SB_T), 1)
    earlier = col < row
    urow = lax.broadcasted_iota(jnp.int32, (2 * SB_T, 2 * SB_T), 0)
    ucol = lax.broadcasted_iota(jnp.int32, (2 * SB_T, 2 * SB_T), 1)
    suffix2 = jnp.where((urow >= ucol) | (ucol >= SB_T), 1.0, 0.0).astype(bf16)[:SB_T]
    suffix_pair = jnp.where((urow >= ucol) & ((urow < SB_T) == (ucol < SB_T)) | (urow >= SB_T) & (ucol < SB_T),
                            1.0, 0.0).astype(bf16)

    first = step == 0
    k_start = [jnp.where(first, 0, step * n_sub - 1) * SB_T] + [
        (step * n_sub + c - 1) * SB_T for c in range(1, n_sub)]
    pair_keys = [pl.ds(pl.multiple_of(k0, SB_T), 2 * SB_T) for k0 in k_start]

    zs = [_dot_nt(q2[c], kn_scr[pair_keys[c], :]) for c in range(n_sub)]
    off = jnp.where(first, 0, SB_T)
    zl = [zs[c][:, :SB_T] for c in range(n_sub)]
    zd = [jnp.where(col + SB_T < row + off if c == 0 else earlier, zs[c][:, SB_T:], NEG_INF)
          for c in range(n_sub)]
    zl[0] = jnp.where(col < row + off, zl[0], NEG_INF)
    zs = [jnp.concatenate([zl[c], zd[c]], axis=1) for c in range(n_sub)]
    rs = [_dot(_softplus2(zs[c]).astype(bf16), suffix_pair) for c in range(n_sub)]

    def still_live(car, next_tile):
        m = jnp.min(jnp.min(car, axis=1, keepdims=True), axis=0, keepdims=True)
        return ((next_tile >= 0) & (m[0, 0] <= SB_DONE_LOG2)).astype(jnp.int32)

    accs, cars, lives = [], [], []
    for c in range(n_sub):
        a = jnp.exp2(zs[c] - rs[c])
        accs.append(_dot(a.astype(bf16), v_ref[pair_keys[c], :]))
        total = rs[c][:, 0:1]
        cars.append(jnp.broadcast_to(total, (2 * SB_T, SB_T)))
        lives.append(still_live(total, step * n_sub + c - 2))
    for c in range(n_sub):
        q2_scr[c] = q2[c]
        acc_scr[c] = accs[c]
        car_scr[c] = cars[c]

    def one_more_tile(c, d):
        kb = step * n_sub + c - d
        keys = pl.ds(pl.multiple_of(kb * SB_T, SB_T), SB_T)
        z = _dot_nt(q2_scr[c], kn_scr[keys, :])
        full = _dot(_softplus2(z).astype(bf16), suffix2)
        car = car_scr[c]
        a = jnp.exp2(z - (full[:, :SB_T] + car))
        acc_scr[c] += _dot(a.astype(bf16), v_ref[keys, :])
        car = car + full[:, SB_T:]
        car_scr[c] = car
        return still_live(car, kb - 1)

    def more(state):
        live = state[1]
        for flag in state[2:]:
            live = live | flag
        return live > 0

    def walk(state):
        d = state[0]
        flags = [lax.cond(state[1 + c] > 0, functools.partial(one_more_tile, c, d), lambda: jnp.int32(0))
                 for c in range(n_sub)]
        return (d + 1, *flags)

    lax.while_loop(more, walk, (jnp.int32(2), *lives))

    for c in range(n_sub):
        o_ref[c * SB_T:(c + 1) * SB_T, :] = jnp.where(
            lo_mask, acc_scr[c, :SB_T, :], acc_scr[c, SB_T:, :]).astype(o_ref.dtype)


def _sb_attn(proj, q_gain, k_gain, batch, seq):
    t = proj.shape[0]
    hpg = D_MODEL // LANES
    n_pairs = SB_HEADS // 2
    n_qt = seq // SB_QROWS
    n_sub = SB_QROWS // SB_T
    return pl.pallas_call(
        _sb_kernel,
        grid=(batch, n_pairs, n_qt),
        in_specs=[
            pl.BlockSpec((SB_QROWS, LANES), lambda b, p, i: (b * n_qt + i, P_SQ * hpg + p)),
            pl.BlockSpec((seq, LANES), lambda b, p, i: (b, P_SK * hpg + p)),
            pl.BlockSpec((seq, LANES), lambda b, p, i: (b, P_SV * hpg + p)),
            pl.BlockSpec((None, 1, LANES), lambda b, p, i: (p, 0, 0)),
            pl.BlockSpec((None, 1, LANES), lambda b, p, i: (p, 0, 0)),
        ],
        out_specs=pl.BlockSpec((SB_QROWS, LANES), lambda b, p, i: (b * n_qt + i, p)),
        out_shape=jax.ShapeDtypeStruct((t, SB_HEADS * SB_DH), bf16),
        scratch_shapes=[
            pltpu.VMEM((seq, LANES), bf16),
            pltpu.VMEM((n_sub, 2 * SB_T, LANES), bf16),
            pltpu.VMEM((n_sub, 2 * SB_T, LANES), f32),
            pltpu.VMEM((n_sub, 2 * SB_T, LANES), f32),
        ],
        compiler_params=pltpu.CompilerParams(
            dimension_semantics=("arbitrary", "arbitrary", "arbitrary"), vmem_limit_bytes=VMEM_LIMIT),
        name="sb_attn",
    )(proj, proj, proj, q_gain.reshape(n_pairs, 1, LANES), k_gain.reshape(n_pairs, 1, LANES))


def _merge_kernel(x_ref, ohg_ref, osb_ref, ga_ref, gb_ref, whg_ref, wsb_ref, wo_ref, o_ref):
    half = x_ref.shape[0] // 2
    parts = [slice(0, half), slice(half, 2 * half)]
    y_hg = [_dot(ohg_ref[r, :], whg_ref[...]) for r in parts]
    y_sb = [_dot(osb_ref[r, :], wsb_ref[...]) for r in parts]
    for r, yh, ys in zip(parts, y_hg, y_sb):
        mixed = ga_ref[r, :].astype(f32) * yh + gb_ref[r, :].astype(f32) * ys
        o_ref[r, :] = x_ref[r, :] + _dot(mixed.astype(bf16), wo_ref[...])


def _merge(x2, o_hg, o_sb, proj, w_hg, w_sb, w_o, tm=1024):
    t = x2.shape[0]
    rows = lambda i: (i, 0)
    resident = dict(pipeline_mode=pl.Buffered(1))
    return pl.pallas_call(
        _merge_kernel,
        grid=(t // tm,),
        in_specs=[
            pl.BlockSpec((tm, D_MODEL), rows),
            pl.BlockSpec((tm, D_MODEL), rows),
            pl.BlockSpec((tm, D_MODEL), rows),
            pl.BlockSpec((tm, D_MODEL), lambda i: (i, P_GA)),
            pl.BlockSpec((tm, D_MODEL), lambda i: (i, P_GB)),
            pl.BlockSpec((D_MODEL, D_MODEL), lambda i: (0, 0), **resident),
            pl.BlockSpec((D_MODEL, D_MODEL), lambda i: (0, 0), **resident),
            pl.BlockSpec((D_MODEL, D_MODEL), lambda i: (0, 0), **resident),
        ],
        out_specs=pl.BlockSpec((tm, D_MODEL), rows),
        out_shape=jax.ShapeDtypeStruct((t, D_MODEL), f32),
        compiler_params=pltpu.CompilerParams(
            dimension_semantics=("arbitrary",), vmem_limit_bytes=VMEM_LIMIT),
        name="merge",
    )(x2, o_hg, o_sb, proj, proj, w_hg, w_sb, w_o)


def _ffn_kernel(x_ref, gain_ref, wi_ref, wd_ref, o_ref, act_scr):
    x = x_ref[...]
    ms = jnp.mean(x * x, axis=-1, keepdims=True)
    h = (x * lax.rsqrt(ms + EPS) * gain_ref[...]).astype(bf16)
    for j in range(FF_HIDDEN // FF_CHUNK):
        cols = slice(j * FF_CHUNK, (j + 1) * FF_CHUNK)
        up_cols = slice(FF_HIDDEN + j * FF_CHUNK, FF_HIDDEN + (j + 1) * FF_CHUNK)
        gate = _dot(h, wi_ref[:, cols])
        up = _dot(h, wi_ref[:, up_cols])
        act_scr[:, cols] = (gate * _sigmoid(gate) * up).astype(bf16)
    o_ref[...] = x + _dot(act_scr[...], wd_ref[...])


def _ffn(x1, gain, w_in_bf, w_out_bf, tm=1024):
    t = x1.shape[0]
    resident = dict(pipeline_mode=pl.Buffered(1))
    return pl.pallas_call(
        _ffn_kernel,
        grid=(t // tm,),
        in_specs=[
            pl.BlockSpec((tm, D_MODEL), lambda i: (i, 0)),
            pl.BlockSpec((1, D_MODEL), lambda i: (0, 0)),
            pl.BlockSpec((D_MODEL, 2 * FF_HIDDEN), lambda i: (0, 0), **resident),
            pl.BlockSpec((FF_HIDDEN, D_MODEL), lambda i: (0, 0), **resident),
        ],
        out_specs=pl.BlockSpec((tm, D_MODEL), lambda i: (i, 0)),
        out_shape=jax.ShapeDtypeStruct((t, D_MODEL), f32),
        scratch_shapes=[pltpu.VMEM((tm, FF_HIDDEN), bf16)],
        compiler_params=pltpu.CompilerParams(
            dimension_semantics=("arbitrary",), vmem_limit_bytes=VMEM_LIMIT),
        name="ffn",
    )(x1, gain, w_in_bf, w_out_bf)


def kernel(x, norm1_gain, w_in, lb_logits, hg_out_norm, sb_q_norm, sb_k_norm,
           w_hg_out, w_sb_out, w_o, norm2_gain, w_ffn_in, w_ffn_out):
    batch, seq, d = x.shape
    depth = norm1_gain.shape[0]
    x2 = x.reshape(batch * seq, d)
    for layer in range(depth):
        gain1 = norm1_gain[layer][None, :]
        proj, h = _in_proj(x2, gain1, w_in[layer])
        g_hi, g_lo = _forget_proj(h, w_in[layer], lb_logits, layer)
        o_hg = _hgrn2(proj, g_hi, g_lo, hg_out_norm[layer], batch, seq)
        o_sb = _sb_attn(proj, sb_q_norm[layer], sb_k_norm[layer], batch, seq)
        x1 = _merge(x2, o_hg, o_sb, proj, w_hg_out[layer].astype(bf16),
                    w_sb_out[layer].astype(bf16), w_o[layer].astype(bf16))
        x2 = _ffn(x1, norm2_gain[layer][None, :], w_ffn_in[layer].astype(bf16),
                  w_ffn_out[layer].astype(bf16))
    return x2.reshape(batch, seq, d)
```

```python
import functools
import math

import jax
import jax.numpy as jnp
from jax import lax
from jax.experimental import pallas as pl
from jax.experimental.pallas import tpu as pltpu

D_MODEL = 1024
HG_HEADS = 8
HG_DK = 128
HG_DV = 128
SB_HEADS = 16
SB_DH = 64
FF_HIDDEN = 2816
FF_CHUNK = 256
EPS = 1e-6
G_HQ, G_HF, G_HI, G_HG, G_SQ, G_SK, G_SV, G_GA, G_GB = range(9)
P_SOURCE = (G_HQ, G_HI, G_HG, G_SQ, G_SK, G_SV, G_GA, G_GB)
P_HQ, P_HI, P_HG, P_SQ, P_SK, P_SV, P_GA, P_GB = range(8)

LANES = 128
HG_CHUNK = 64
HG_PAR = 4
HG_ROWS = 1024
HG_EXP_CLAMP = 80.0
SB_T = 128
SB_QROWS = 2048
SB_DONE_LOG2 = 128.0
VMEM_LIMIT = 56 * 1024 * 1024

f32 = jnp.float32
bf16 = jnp.bfloat16
NEG_INF = float("-inf")


def _sigmoid(x):
    return 1.0 / (1.0 + jnp.exp(-x))


def _sigmoid_tanh(x):
    return 0.5 * jnp.tanh(0.5 * x) + 0.5


def _dot(a, b):
    return jnp.dot(a, b, preferred_element_type=f32)


def _dot_nt(a, b):
    return lax.dot_general(a, b, (((1,), (1,)), ((), ())), preferred_element_type=f32)


def _dot_tn(a, b):
    return lax.dot_general(a, b, (((0,), (0,)), ((), ())), preferred_element_type=f32)


def _split2(x):
    hi = x.astype(bf16)
    lo = (x - hi.astype(f32)).astype(bf16)
    return hi, lo


def _rms_norm_bf16(x, gain):
    ms = jnp.mean(x * x, axis=-1, keepdims=True)
    return (x * lax.rsqrt(ms + EPS) * gain).astype(bf16)


def _inproj_kernel(x_ref, gain_ref, w_ref, o_ref, h_ref):
    j = pl.program_id(1)

    @pl.when(j == 0)
    def _():
        w = w_ref[...].astype(bf16)
        n_blk = 4
        blk = x_ref.shape[0] // n_blk
        for b in range(n_blk):
            rows = slice(b * blk, (b + 1) * blk)
            h = _rms_norm_bf16(x_ref[rows, :], gain_ref[...])
            h_ref[rows, :] = h
            o_ref[rows, :] = _dot(h, w).astype(o_ref.dtype)

    is_gate = (j == P_HG) | (j == P_GA) | (j == P_GB)

    @pl.when(is_gate)
    def _():
        o_ref[...] = _sigmoid_tanh(_dot(h_ref[...], w_ref[...].astype(bf16))).astype(o_ref.dtype)

    @pl.when(jnp.logical_not(is_gate) & (j > 0))
    def _():
        o_ref[...] = _dot(h_ref[...], w_ref[...].astype(bf16)).astype(o_ref.dtype)


def _in_proj(x2, gain, w_in, tm=2048):
    t = x2.shape[0]
    n_groups = len(P_SOURCE)
    return pl.pallas_call(
        _inproj_kernel,
        grid=(t // tm, n_groups),
        in_specs=[
            pl.BlockSpec((tm, D_MODEL), lambda i, j: (i, 0)),
            pl.BlockSpec((1, D_MODEL), lambda i, j: (0, 0)),
            pl.BlockSpec((D_MODEL, D_MODEL), lambda i, j: (0, jnp.where(j >= G_HF, j + 1, j))),
        ],
        out_specs=[
            pl.BlockSpec((tm, D_MODEL), lambda i, j: (i, j)),
            pl.BlockSpec((tm, D_MODEL), lambda i, j: (i, 0)),
        ],
        out_shape=[
            jax.ShapeDtypeStruct((t, n_groups * D_MODEL), bf16),
            jax.ShapeDtypeStruct((t, D_MODEL), bf16),
        ],
        compiler_params=pltpu.CompilerParams(
            dimension_semantics=("arbitrary", "arbitrary"), vmem_limit_bytes=VMEM_LIMIT),
        name="in_proj",
    )(x2, gain, w_in)


def _forget_kernel(h_ref, w_ref, lbl_ref, ghi_ref, glo_ref, k_ref, *, layer):
    l = lbl_ref[...]
    e = jnp.exp(l - jnp.max(l, axis=0, keepdims=True))
    lb = jnp.sum(e[0:layer + 1, :], axis=0, keepdims=True) / jnp.sum(e, axis=0, keepdims=True)
    f_mid = 0.5 * (1.0 + lb)
    f_amp = 0.5 * (1.0 - lb)
    w = w_ref[...].astype(bf16)
    n_blk = 4
    blk = h_ref.shape[0] // n_blk
    ys = [_dot(h_ref[b * blk:(b + 1) * blk, :], w) for b in range(n_blk)]
    for b in range(n_blk):
        rows = slice(b * blk, (b + 1) * blk)
        f = f_mid + f_amp * jnp.tanh(0.5 * ys[b])
        ghi_ref[rows, :], glo_ref[rows, :] = _split2(jnp.log(f))
        k_ref[rows, :] = (1.0 - f).astype(k_ref.dtype)


def _forget_proj(h, w_in, lb_logits, layer, tm=1024):
    t = h.shape[0]
    rows = lambda i: (i, 0)
    out = jax.ShapeDtypeStruct((t, D_MODEL), bf16)
    return pl.pallas_call(
        functools.partial(_forget_kernel, layer=layer),
        grid=(t // tm,),
        in_specs=[
            pl.BlockSpec((tm, D_MODEL), rows),
            pl.BlockSpec((D_MODEL, D_MODEL), lambda i: (0, G_HF)),
            pl.BlockSpec(lb_logits.shape, lambda i: (0, 0)),
        ],
        out_specs=[pl.BlockSpec((tm, D_MODEL), rows)] * 3,
        out_shape=[out] * 3,
        compiler_params=pltpu.CompilerParams(
            dimension_semantics=("arbitrary",), vmem_limit_bytes=VMEM_LIMIT),
        name="forget_proj",
    )(h, w_in, lb_logits)


def _hgrn_kernel(q_ref, ghi_ref, glo_ref, k_ref, i_ref, og_ref, gain_ref, o_ref, st_scr):
    c_len = HG_CHUNK
    n_blk = q_ref.shape[0] // (HG_PAR * c_len)

    @pl.when(pl.program_id(1) == 0)
    def _():
        st_scr[...] = jnp.zeros_like(st_scr)

    gain = gain_ref[...]

    row = lax.broadcasted_iota(jnp.int32, (c_len, c_len), 0)
    col = lax.broadcasted_iota(jnp.int32, (c_len, c_len), 1)
    causal = row >= col
    tri = jnp.where(causal, 1.0, 0.0).astype(bf16)
    tri2 = jnp.concatenate([tri, tri], axis=1)
    heads = range(HG_HEADS)
    sl = [slice(h * HG_DK, (h + 1) * HG_DK) for h in heads]

    def chunk_operands(c):
        rows = pl.ds(pl.multiple_of(c * c_len, c_len), c_len)
        cum = _dot(tri2, jnp.concatenate([ghi_ref[rows, :], glo_ref[rows, :]], axis=0))
        k = k_ref[rows, :].astype(f32)
        last = cum[c_len - 1:c_len, :]
        mid = cum[c_len // 2 - 1:c_len // 2, :]
        q = q_ref[rows, :].astype(f32)
        return dict(
            rows=rows,
            qt=(q * jnp.exp(jnp.minimum(cum - mid, HG_EXP_CLAMP))).astype(bf16),
            kt=(k * jnp.exp(jnp.minimum(mid - cum, HG_EXP_CLAMP))).astype(bf16),
            qd=(q * jnp.exp(cum)).astype(bf16),
            kd=(k * jnp.exp(last - cum)).astype(bf16),
            e_last=jnp.exp(last),
            v=i_ref[rows, :],
            out_scale=gain * og_ref[rows, :].astype(f32),
        )

    def body(it, carry):
        cs = [chunk_operands(it * HG_PAR + p) for p in range(HG_PAR)]
        scores =[[jnp.where(causal, _dot_nt(c["qt"][:, sl[h]], c["kt"][:, sl[h]]), 0.0).astype(bf16)
                   for h in heads] for c in cs]
        upd = [[_dot_tn(c["v"][:, sl[h]], c["kd"][:, sl[h]]) for h in heads] for c in cs]
        states = [st_scr[h] for h in heads]
        inter = []
        for p, c in enumerate(cs):
            inter.append([_dot_nt(c["qd"][:, sl[h]], states[h].astype(bf16)) for h in heads])
            states = [states[h] * c["e_last"][:, sl[h]] + upd[p][h] for h in heads]
        for h in heads:
            st_scr[h] = states[h]
        for p, c in enumerate(cs):
            outs = []
            for h in heads:
                o = _dot(scores[p][h], c["v"][:, sl[h]]) + inter[p][h]
                ms = jnp.mean(o * o, axis=-1, keepdims=True)
                outs.append((o * lax.rsqrt(ms + EPS) * c["out_scale"][:, sl[h]]).astype(o_ref.dtype))
            o_ref[c["rows"], :] = jnp.concatenate(outs, axis=1)
        return carry

    lax.fori_loop(0, n_blk, body, 0)


def _hgrn2(proj, g_hi, g_lo, one_minus_f, hg_gain, batch, seq):
    t = proj.shape[0]
    n_rt = seq // HG_ROWS

    def grp(group):
        return pl.BlockSpec((HG_ROWS, D_MODEL), lambda b, r: (b * n_rt + r, group))

    return pl.pallas_call(
        _hgrn_kernel,
        grid=(batch, n_rt),
        in_specs=[
            grp(P_HQ), grp(0), grp(0), grp(0), grp(P_HI), grp(P_HG),
            pl.BlockSpec((1, D_MODEL), lambda b, r: (0, 0)),
        ],
        out_specs=grp(0),
        out_shape=jax.ShapeDtypeStruct((t, HG_HEADS * HG_DV), bf16),
        scratch_shapes=[pltpu.VMEM((HG_HEADS, HG_DV, HG_DK), f32)],
        compiler_params=pltpu.CompilerParams(
            dimension_semantics=("arbitrary", "arbitrary"), vmem_limit_bytes=VMEM_LIMIT),
        name="hgrn2",
    )(proj, g_hi, g_lo, one_minus_f, proj, proj, hg_gain.reshape(1, HG_HEADS * HG_DV))


def _pair_rms(blocks, gain, group_mean):
    ms = [_dot((x * x).astype(bf16), group_mean) for x in blocks]
    return [x * lax.rsqrt(m + EPS) * gain for x, m in zip(blocks, ms)]


def _softplus2(z):
    return jnp.maximum(z, 0.0) + jnp.log2(1.0 + jnp.exp2(-jnp.abs(z)))


def _sb_kernel(q_ref, k_ref, v_ref, qg_ref, kg_ref, o_ref, kn_scr, q2_scr, acc_scr, car_scr):
    step = pl.program_id(2)
    seq = k_ref.shape[0]
    n_sub = SB_QROWS // SB_T
    lane = lax.broadcasted_iota(jnp.int32, (1, LANES), 1)
    lo_mask = lane < SB_DH

    gr = lax.broadcasted_iota(jnp.int32, (LANES, LANES), 0) // SB_DH
    gc = lax.broadcasted_iota(jnp.int32, (LANES, LANES), 1) // SB_DH
    group_mean = jnp.where(gr == gc, 1.0 / SB_DH, 0.0).astype(bf16)

    @pl.when(step == 0)
    def _():
        prep = 512

        blocks = [k_ref[r * prep:(r + 1) * prep, :].astype(f32) for r in range(seq // prep)]
        for r, kn in enumerate(_pair_rms(blocks, kg_ref[...], group_mean)):
            kn_scr[r * prep:(r + 1) * prep, :] = kn.astype(bf16)

    q_scale = math.log2(math.e) / math.sqrt(SB_DH)
    qn = _pair_rms([q_ref[c * SB_T:(c + 1) * SB_T, :].astype(f32) for c in range(n_sub)],
                   qg_ref[...] * q_scale, group_mean)
    q2 = [jnp.concatenate([jnp.where(lo_mask, q, 0.0).astype(bf16),
                           jnp.where(lo_mask, 0.0, q).astype(bf16)], axis=0) for q in qn]

    row = lax.broadcasted_iota(jnp.int32, (2 * SB_T, SB_T), 0) % SB_T
    col = lax.broadcasted_iota(jnp.int32, (2 * SB_T, SB_T), 1)
    earlier = col < row
    urow = lax.broadcasted_iota(jnp.int32, (2 * SB_T, 2 * SB_T), 0)
    ucol = lax.broadcasted_iota(jnp.int32, (2 * SB_T, 2 * SB_T), 1)
    suffix2 = jnp.where((urow >= ucol) | (ucol >= SB_T), 1.0, 0.0).astype(bf16)[:SB_T]
    suffix_pair = jnp.where((urow >= ucol) & ((urow < SB_T) == (ucol < SB_T)) | (urow >= SB_T) & (ucol < SB_T),
                            1.0, 0.0).astype(bf16)

    first = step == 0
    k_start = [jnp.where(first, 0, step * n_sub - 1) * SB_T] + [
        (step * n_sub + c - 1) * SB_T for c in range(1, n_sub)]
    pair_keys = [pl.ds(pl.multiple_of(k0, SB_T), 2 * SB_T) for k0 in k_start]

    zs = [_dot_nt(q2[c], kn_scr[pair_keys[c], :]) for c in range(n_sub)]
    off = jnp.where(first, 0, SB_T)
    zl = [zs[c][:, :SB_T] for c in range(n_sub)]
    zd = [jnp.where(col + SB_T < row + off if c == 0 else earlier, zs[c][:, SB_T:], NEG_INF)
          for c in range(n_sub)]
    zl[0] = jnp.where(col < row + off, zl[0], NEG_INF)
    zs = [jnp.concatenate([zl[c], zd[c]], axis=1) for c in range(n_sub)]
    rs = [_dot(_softplus2(zs[c]).astype(bf16), suffix_pair) for c in range(n_sub)]

    def still_live(car, next_tile):
        m = jnp.min(jnp.min(car, axis=1, keepdims=True), axis=0, keepdims=True)
        return ((next_tile >= 0) & (m[0, 0] <= SB_DONE_LOG2)).astype(jnp.int32)

    accs, cars, lives = [], [], []
    for c in range(n_sub):
        a = jnp.exp2(zs[c] - rs[c])
        accs.append(_dot(a.astype(bf16), v_ref[pair_keys[c], :]))
        total = rs[c][:, 0:1]
        cars.append(jnp.broadcast_to(total, (2 * SB_T, SB_T)))
        lives.append(still_live(total, step * n_sub + c - 2))
    for c in range(n_sub):
        q2_scr[c] = q2[c]
        acc_scr[c] = accs[c]
        car_scr[c] = cars[c]

    def one_more_tile(c, d):
        kb = step * n_sub + c - d
        keys = pl.ds(pl.multiple_of(kb * SB_T, SB_T), SB_T)
        z = _dot_nt(q2_scr[c], kn_scr[keys, :])
        full = _dot(_softplus2(z).astype(bf16), suffix2)
        car = car_scr[c]
        a = jnp.exp2(z - (full[:, :SB_T] + car))
        acc_scr[c] += _dot(a.astype(bf16), v_ref[keys, :])
        car = car + full[:, SB_T:]
        car_scr[c] = car
        return still_live(car, kb - 1)

    def more(state):
        live = state[1]
        for flag in state[2:]:
            live = live | flag
        return live > 0

    def walk(state):
        d = state[0]
        flags = [lax.cond(state[1 + c] > 0, functools.partial(one_more_tile, c, d), lambda: jnp.int32(0))
                 for c in range(n_sub)]
        return (d + 1, *flags)

    lax.while_loop(more, walk, (jnp.int32(2), *lives))

    for c in range(n_sub):
        o_ref[c * SB_T:(c + 1) * SB_T, :] = jnp.where(
            lo_mask, acc_scr[c, :SB_T, :], acc_scr[c, SB_T:, :]).astype(o_ref.dtype)


def _sb_attn(proj, q_gain, k_gain, batch, seq):
    t = proj.shape[0]
    hpg = D_MODEL // LANES
    n_pairs = SB_HEADS // 2
    n_qt = seq // SB_QROWS
    n_sub = SB_QROWS // SB_T
    return pl.pallas_call(
        _sb_kernel,
        grid=(batch, n_pairs, n_qt),
        in_specs=[
            pl.BlockSpec((SB_QROWS, LANES), lambda b, p, i: (b * n_qt + i, P_SQ * hpg + p)),
            pl.BlockSpec((seq, LANES), lambda b, p, i: (b, P_SK * hpg + p)),
            pl.BlockSpec((seq, LANES), lambda b, p, i: (b, P_SV * hpg + p)),
            pl.BlockSpec((None, 1, LANES), lambda b, p, i: (p, 0, 0)),
            pl.BlockSpec((None, 1, LANES), lambda b, p, i: (p, 0, 0)),
        ],
        out_specs=pl.BlockSpec((SB_QROWS, LANES), lambda b, p, i: (b * n_qt + i, p)),
        out_shape=jax.ShapeDtypeStruct((t, SB_HEADS * SB_DH), bf16),
        scratch_shapes=[
            pltpu.VMEM((seq, LANES), bf16),
            pltpu.VMEM((n_sub, 2 * SB_T, LANES), bf16),
            pltpu.VMEM((n_sub, 2 * SB_T, LANES), f32),
            pltpu.VMEM((n_sub, 2 * SB_T, LANES), f32),
        ],
        compiler_params=pltpu.CompilerParams(
            dimension_semantics=("arbitrary", "arbitrary", "arbitrary"), vmem_limit_bytes=VMEM_LIMIT),
        name="sb_attn",
    )(proj, proj, proj, q_gain.reshape(n_pairs, 1, LANES), k_gain.reshape(n_pairs, 1, LANES))


def _merge_kernel(x_ref, ohg_ref, osb_ref, ga_ref, gb_ref, whg_ref, wsb_ref, wo_ref, o_ref):
    half = x_ref.shape[0] // 2
    parts = [slice(0, half), slice(half, 2 * half)]
    y_hg = [_dot(ohg_ref[r, :], whg_ref[...]) for r in parts]
    y_sb = [_dot(osb_ref[r, :], wsb_ref[...]) for r in parts]
    for r, yh, ys in zip(parts, y_hg, y_sb):
        mixed = ga_ref[r, :].astype(f32) * yh + gb_ref[r, :].astype(f32) * ys
        o_ref[r, :] = x_ref[r, :] + _dot(mixed.astype(bf16), wo_ref[...])


def _merge(x2, o_hg, o_sb, proj, w_hg, w_sb, w_o, tm=1024):
    t = x2.shape[0]
    rows = lambda i: (i, 0)
    resident = dict(pipeline_mode=pl.Buffered(1))
    return pl.pallas_call(
        _merge_kernel,
        grid=(t // tm,),
        in_specs=[
            pl.BlockSpec((tm, D_MODEL), rows),
            pl.BlockSpec((tm, D_MODEL), rows),
            pl.BlockSpec((tm, D_MODEL), rows),
            pl.BlockSpec((tm, D_MODEL), lambda i: (i, P_GA)),
            pl.BlockSpec((tm, D_MODEL), lambda i: (i, P_GB)),
            pl.BlockSpec((D_MODEL, D_MODEL), lambda i: (0, 0), **resident),
            pl.BlockSpec((D_MODEL, D_MODEL), lambda i: (0, 0), **resident),
            pl.BlockSpec((D_MODEL, D_MODEL), lambda i: (0, 0), **resident),
        ],
        out_specs=pl.BlockSpec((tm, D_MODEL), rows),
        out_shape=jax.ShapeDtypeStruct((t, D_MODEL), f32),
        compiler_params=pltpu.CompilerParams(
            dimension_semantics=("arbitrary",), vmem_limit_bytes=VMEM_LIMIT),
        name="merge",
    )(x2, o_hg, o_sb, proj, proj, w_hg, w_sb, w_o)


def _ffn_kernel(x_ref, gain_ref, wi_ref, wd_ref, o_ref, act_scr):
    x = x_ref[...]
    ms = jnp.mean(x * x, axis=-1, keepdims=True)
    h = (x * lax.rsqrt(ms + EPS) * gain_ref[...]).astype(bf16)
    for j in range(FF_HIDDEN // FF_CHUNK):
        cols = slice(j * FF_CHUNK, (j + 1) * FF_CHUNK)
        up_cols = slice(FF_HIDDEN + j * FF_CHUNK, FF_HIDDEN + (j + 1) * FF_CHUNK)
        gate = _dot(h, wi_ref[:, cols])
        up = _dot(h, wi_ref[:, up_cols])
        act_scr[:, cols] = (gate * _sigmoid(gate) * up).astype(bf16)
    o_ref[...] = x + _dot(act_scr[...], wd_ref[...])


def _ffn(x1, gain, w_in_bf, w_out_bf, tm=1024):
    t = x1.shape[0]
    resident = dict(pipeline_mode=pl.Buffered(1))
    return pl.pallas_call(
        _ffn_kernel,
        grid=(t // tm,),
        in_specs=[
            pl.BlockSpec((tm, D_MODEL), lambda i: (i, 0)),
            pl.BlockSpec((1, D_MODEL), lambda i: (0, 0)),
            pl.BlockSpec((D_MODEL, 2 * FF_HIDDEN), lambda i: (0, 0), **resident),
            pl.BlockSpec((FF_HIDDEN, D_MODEL), lambda i: (0, 0), **resident),
        ],
        out_specs=pl.BlockSpec((tm, D_MODEL), lambda i: (i, 0)),
        out_shape=jax.ShapeDtypeStruct((t, D_MODEL), f32),
        scratch_shapes=[pltpu.VMEM((tm, FF_HIDDEN), bf16)],
        compiler_params=pltpu.CompilerParams(
            dimension_semantics=("arbitrary",), vmem_limit_bytes=VMEM_LIMIT),
        name="ffn",
    )(x1, gain, w_in_bf, w_out_bf)


def kernel(x, norm1_gain, w_in, lb_logits, hg_out_norm, sb_q_norm, sb_k_norm,
           w_hg_out, w_sb_out, w_o, norm2_gain, w_ffn_in, w_ffn_out):
    batch, seq, d = x.shape
    depth = norm1_gain.shape[0]
    x2 = x.reshape(batch * seq, d)
    for layer in range(depth):
        gain1 = norm1_gain[layer][None, :]
        proj, h = _in_proj(x2, gain1, w_in[layer])
        g_hi, g_lo, one_minus_f = _forget_proj(h, w_in[layer], lb_logits, layer)
        o_hg = _hgrn2(proj, g_hi, g_lo, one_minus_f, hg_out_norm[layer], batch, seq)
        o_sb = _sb_attn(proj, sb_q_norm[layer], sb_k_norm[layer], batch, seq)
        x1 = _merge(x2, o_hg, o_sb, proj, w_hg_out[layer].astype(bf16),
                    w_sb_out[layer].astype(bf16), w_o[layer].astype(bf16))
        x2 = _ffn(x1, norm2_gain[layer][None, :], w_ffn_in[layer].astype(bf16),
                  w_ffn_out[layer].astype(bf16))
    return x2.reshape(batch, seq, d)
```

```python
import functools
import math

import jax
import jax.numpy as jnp
from jax import lax
from jax.experimental import pallas as pl
from jax.experimental.pallas import tpu as pltpu

D_MODEL = 1024
HG_HEADS = 8
HG_DK = 128
HG_DV = 128
SB_HEADS = 16
SB_DH = 64
FF_HIDDEN = 2816
FF_CHUNK = 256
EPS = 1e-6
G_HQ, G_HF, G_HI, G_HG, G_SQ, G_SK, G_SV, G_GA, G_GB = range(9)
P_SOURCE = (G_HQ, G_HI, G_HG, G_SQ, G_SK, G_SV, G_GA, G_GB)
P_HQ, P_HI, P_HG, P_SQ, P_SK, P_SV, P_GA, P_GB = range(8)

LANES = 128
HG_CHUNK = 64
HG_PAR = 4
HG_ROWS = 2048
HG_EXP_CLAMP = 80.0
SB_T = 128
SB_QROWS = 2048
SB_DONE_LOG2 = 128.0
VMEM_LIMIT = 56 * 1024 * 1024

f32 = jnp.float32
bf16 = jnp.bfloat16
NEG_INF = float("-inf")


def _sigmoid(x):
    return 1.0 / (1.0 + jnp.exp(-x))


def _sigmoid_tanh(x):
    return 0.5 * jnp.tanh(0.5 * x) + 0.5


def _dot(a, b):
    return jnp.dot(a, b, preferred_element_type=f32)


def _dot_nt(a, b):
    return lax.dot_general(a, b, (((1,), (1,)), ((), ())), preferred_element_type=f32)


def _dot_tn(a, b):
    return lax.dot_general(a, b, (((0,), (0,)), ((), ())), preferred_element_type=f32)


def _split2(x):
    hi = x.astype(bf16)
    lo = (x - hi.astype(f32)).astype(bf16)
    return hi, lo


def _rms_norm_bf16(x, gain):
    ms = jnp.mean(x * x, axis=-1, keepdims=True)
    return (x * lax.rsqrt(ms + EPS) * gain).astype(bf16)


def _inproj_kernel(x_ref, gain_ref, w_ref, o_ref, h_ref):
    j = pl.program_id(1)

    @pl.when(j == 0)
    def _():
        w = w_ref[...].astype(bf16)
        n_blk = 4
        blk = x_ref.shape[0] // n_blk
        for b in range(n_blk):
            rows = slice(b * blk, (b + 1) * blk)
            h = _rms_norm_bf16(x_ref[rows, :], gain_ref[...])
            h_ref[rows, :] = h
            o_ref[rows, :] = _dot(h, w).astype(o_ref.dtype)

    is_gate = (j == P_HG) | (j == P_GA) | (j == P_GB)

    @pl.when(is_gate)
    def _():
        o_ref[...] = _sigmoid_tanh(_dot(h_ref[...], w_ref[...].astype(bf16))).astype(o_ref.dtype)

    @pl.when(jnp.logical_not(is_gate) & (j > 0))
    def _():
        o_ref[...] = _dot(h_ref[...], w_ref[...].astype(bf16)).astype(o_ref.dtype)


def _in_proj(x2, gain, w_in, tm=2048):
    t = x2.shape[0]
    n_groups = len(P_SOURCE)
    return pl.pallas_call(
        _inproj_kernel,
        grid=(t // tm, n_groups),
        in_specs=[
            pl.BlockSpec((tm, D_MODEL), lambda i, j: (i, 0)),
            pl.BlockSpec((1, D_MODEL), lambda i, j: (0, 0)),
            pl.BlockSpec((D_MODEL, D_MODEL), lambda i, j: (0, jnp.where(j >= G_HF, j + 1, j))),
        ],
        out_specs=[
            pl.BlockSpec((tm, D_MODEL), lambda i, j: (i, j)),
            pl.BlockSpec((tm, D_MODEL), lambda i, j: (i, 0)),
        ],
        out_shape=[
            jax.ShapeDtypeStruct((t, n_groups * D_MODEL), bf16),
            jax.ShapeDtypeStruct((t, D_MODEL), bf16),
        ],
        compiler_params=pltpu.CompilerParams(
            dimension_semantics=("arbitrary", "arbitrary"), vmem_limit_bytes=VMEM_LIMIT),
        name="in_proj",
    )(x2, gain, w_in)


def _hgrn_kernel(q_ref, h_ref, i_ref, og_ref, w_ref, lbl_ref, gain_ref, o_ref, st_scr, w_scr, fg_scr,
                 *, layer):
    c_len = HG_CHUNK
    blk = HG_PAR * c_len
    n_blk = q_ref.shape[0] // blk

    @pl.when((pl.program_id(0) == 0) & (pl.program_id(1) == 0))
    def _():
        w_scr[...] = w_ref[...].astype(bf16)

    @pl.when(pl.program_id(1) == 0)
    def _():
        st_scr[...] = jnp.zeros_like(st_scr)

    l = lbl_ref[...]
    e = jnp.exp(l - jnp.max(l, axis=0, keepdims=True))
    lb = jnp.sum(e[0:layer + 1, :], axis=0, keepdims=True) / jnp.sum(e, axis=0, keepdims=True)
    f_mid = 0.5 * (1.0 + lb)
    f_amp = 0.5 * (1.0 - lb)
    gain = gain_ref[...]

    def forget_block(it, slot):
        rows = pl.ds(pl.multiple_of(it * blk, blk), blk)
        f = f_mid + f_amp * jnp.tanh(0.5 * _dot(h_ref[rows, :], w_scr[...]))
        fg_scr[slot, 0], fg_scr[slot, 1] = _split2(jnp.log(f))
        fg_scr[slot, 2] = (1.0 - f).astype(bf16)

    row = lax.broadcasted_iota(jnp.int32, (c_len, c_len), 0)
    col = lax.broadcasted_iota(jnp.int32, (c_len, c_len), 1)
    causal = row >= col
    tri = jnp.where(causal, 1.0, 0.0).astype(bf16)
    tri2 = jnp.concatenate([tri, tri], axis=1)
    heads = range(HG_HEADS)
    sl = [slice(h * HG_DK, (h + 1) * HG_DK) for h in heads]

    def chunk_operands(it, p, slot):
        rows = pl.ds(pl.multiple_of(it * blk + p * c_len, c_len), c_len)
        in_blk = slice(p * c_len, (p + 1) * c_len)
        cum = _dot(tri2, jnp.concatenate([fg_scr[slot, 0, in_blk, :], fg_scr[slot, 1, in_blk, :]], axis=0))
        k = fg_scr[slot, 2, in_blk, :].astype(f32)
        last = cum[c_len - 1:c_len, :]
        mid = cum[c_len // 2 - 1:c_len // 2, :]
        q = q_ref[rows, :].astype(f32)
        return dict(
            rows=rows,
            qt=(q * jnp.exp(jnp.minimum(cum - mid, HG_EXP_CLAMP))).astype(bf16),
            kt=(k * jnp.exp(jnp.minimum(mid - cum, HG_EXP_CLAMP))).astype(bf16),
            qd=(q * jnp.exp(cum)).astype(bf16),
            kd=(k * jnp.exp(last - cum)).astype(bf16),
            e_last=jnp.exp(last),
            v=i_ref[rows, :],
            out_scale=gain * og_ref[rows, :].astype(f32),
        )

    def body(it, carry):
        slot = it % 2
        cs = [chunk_operands(it, p, slot) for p in range(HG_PAR)]
        forget_block(jnp.minimum(it + 1, n_blk - 1), 1 - slot)
        scores =[[jnp.where(causal, _dot_nt(c["qt"][:, sl[h]], c["kt"][:, sl[h]]), 0.0).astype(bf16)
                   for h in heads] for c in cs]
        upd = [[_dot_tn(c["v"][:, sl[h]], c["kd"][:, sl[h]]) for h in heads] for c in cs]
        states = [st_scr[h] for h in heads]
        inter = []
        for p, c in enumerate(cs):
            inter.append([_dot_nt(c["qd"][:, sl[h]], states[h].astype(bf16)) for h in heads])
            states = [states[h] * c["e_last"][:, sl[h]] + upd[p][h] for h in heads]
        for h in heads:
            st_scr[h] = states[h]
        for p, c in enumerate(cs):
            outs = []
            for h in heads:
                o = _dot(scores[p][h], c["v"][:, sl[h]]) + inter[p][h]
                ms = jnp.mean(o * o, axis=-1, keepdims=True)
                outs.append((o * lax.rsqrt(ms + EPS) * c["out_scale"][:, sl[h]]).astype(o_ref.dtype))
            o_ref[c["rows"], :] = jnp.concatenate(outs, axis=1)
        return carry

    forget_block(0, 0)
    lax.fori_loop(0, n_blk, body, 0)


def _hgrn2(proj, h, w_in, lb_logits, hg_gain, layer, batch, seq):
    t = proj.shape[0]
    n_rt = seq // HG_ROWS

    def grp(group):
        return pl.BlockSpec((HG_ROWS, D_MODEL), lambda b, r: (b * n_rt + r, group))

    return pl.pallas_call(
        functools.partial(_hgrn_kernel, layer=layer),
        grid=(batch, n_rt),
        in_specs=[
            grp(P_HQ), grp(0), grp(P_HI), grp(P_HG),
            pl.BlockSpec((D_MODEL, D_MODEL), lambda b, r: (0, G_HF), pipeline_mode=pl.Buffered(1)),
            pl.BlockSpec(lb_logits.shape, lambda b, r: (0, 0)),
            pl.BlockSpec((1, D_MODEL), lambda b, r: (0, 0)),
        ],
        out_specs=grp(0),
        out_shape=jax.ShapeDtypeStruct((t, HG_HEADS * HG_DV), bf16),
        scratch_shapes=[
            pltpu.VMEM((HG_HEADS, HG_DV, HG_DK), f32),
            pltpu.VMEM((D_MODEL, D_MODEL), bf16),
            pltpu.VMEM((2, 3, HG_PAR * HG_CHUNK, D_MODEL), bf16),
        ],
        compiler_params=pltpu.CompilerParams(
            dimension_semantics=("arbitrary", "arbitrary"), vmem_limit_bytes=VMEM_LIMIT),
        name="hgrn2",
    )(proj, h, proj, proj, w_in, lb_logits, hg_gain.reshape(1, HG_HEADS * HG_DV))


def _pair_rms(blocks, gain, group_mean):
    ms = [_dot((x * x).astype(bf16), group_mean) for x in blocks]
    return [x * lax.rsqrt(m + EPS) * gain for x, m in zip(blocks, ms)]


def _softplus2(z):
    return jnp.maximum(z, 0.0) + jnp.log2(1.0 + jnp.exp2(-jnp.abs(z)))


def _sb_kernel(q_ref, k_ref, v_ref, qg_ref, kg_ref, o_ref, kn_scr, q2_scr, acc_scr, car_scr):
    step = pl.program_id(2)
    seq = k_ref.shape[0]
    n_sub = SB_QROWS // SB_T
    lane = lax.broadcasted_iota(jnp.int32, (1, LANES), 1)
    lo_mask = lane < SB_DH

    gr = lax.broadcasted_iota(jnp.int32, (LANES, LANES), 0) // SB_DH
    gc = lax.broadcasted_iota(jnp.int32, (LANES, LANES), 1) // SB_DH
    group_mean = jnp.where(gr == gc, 1.0 / SB_DH, 0.0).astype(bf16)

    @pl.when(step == 0)
    def _():
        prep = 512

        blocks = [k_ref[r * prep:(r + 1) * prep, :].astype(f32) for r in range(seq // prep)]
        for r, kn in enumerate(_pair_rms(blocks, kg_ref[...], group_mean)):
            kn_scr[r * prep:(r + 1) * prep, :] = kn.astype(bf16)

    q_scale = math.log2(math.e) / math.sqrt(SB_DH)
    qn = _pair_rms([q_ref[c * SB_T:(c + 1) * SB_T, :].astype(f32) for c in range(n_sub)],
                   qg_ref[...] * q_scale, group_mean)
    q2 = [jnp.concatenate([jnp.where(lo_mask, q, 0.0).astype(bf16),
                           jnp.where(lo_mask, 0.0, q).astype(bf16)], axis=0) for q in qn]

    row = lax.broadcasted_iota(jnp.int32, (2 * SB_T, SB_T), 0) % SB_T
    col = lax.broadcasted_iota(jnp.int32, (2 * SB_T, SB_T), 1)
    earlier = col < row
    urow = lax.broadcasted_iota(jnp.int32, (2 * SB_T, 2 * SB_T), 0)
    ucol = lax.broadcasted_iota(jnp.int32, (2 * SB_T, 2 * SB_T), 1)
    suffix2 = jnp.where((urow >= ucol) | (ucol >= SB_T), 1.0, 0.0).astype(bf16)[:SB_T]
    suffix_pair = jnp.where((urow >= ucol) & ((urow < SB_T) == (ucol < SB_T)) | (urow >= SB_T) & (ucol < SB_T),
                            1.0, 0.0).astype(bf16)

    first = step == 0
    k_start = [jnp.where(first, 0, step * n_sub - 1) * SB_T] + [
        (step * n_sub + c - 1) * SB_T for c in range(1, n_sub)]
    pair_keys = [pl.ds(pl.multiple_of(k0, SB_T), 2 * SB_T) for k0 in k_start]

    zs = [_dot_nt(q2[c], kn_scr[pair_keys[c], :]) for c in range(n_sub)]
    off = jnp.where(first, 0, SB_T)
    zl = [zs[c][:, :SB_T] for c in range(n_sub)]
    zd = [jnp.where(col + SB_T < row + off if c == 0 else earlier, zs[c][:, SB_T:], NEG_INF)
          for c in range(n_sub)]
    zl[0] = jnp.where(col < row + off, zl[0], NEG_INF)
    zs = [jnp.concatenate([zl[c], zd[c]], axis=1) for c in range(n_sub)]
    rs = [_dot(_softplus2(zs[c]).astype(bf16), suffix_pair) for c in range(n_sub)]

    def still_live(car, next_tile):
        m = jnp.min(jnp.min(car, axis=1, keepdims=True), axis=0, keepdims=True)
        return ((next_tile >= 0) & (m[0, 0] <= SB_DONE_LOG2)).astype(jnp.int32)

    accs, cars, lives = [], [], []
    for c in range(n_sub):
        a = jnp.exp2(zs[c] - rs[c])
        accs.append(_dot(a.astype(bf16), v_ref[pair_keys[c], :]))
        total = rs[c][:, 0:1]
        cars.append(jnp.broadcast_to(total, (2 * SB_T, SB_T)))
        lives.append(still_live(total, step * n_sub + c - 2))
    for c in range(n_sub):
        q2_scr[c] = q2[c]
        acc_scr[c] = accs[c]
        car_scr[c] = cars[c]

    def one_more_tile(c, d):
        kb = step * n_sub + c - d
        keys = pl.ds(pl.multiple_of(kb * SB_T, SB_T), SB_T)
        z = _dot_nt(q2_scr[c], kn_scr[keys, :])
        full = _dot(_softplus2(z).astype(bf16), suffix2)
        car = car_scr[c]
        a = jnp.exp2(z - (full[:, :SB_T] + car))
        acc_scr[c] += _dot(a.astype(bf16), v_ref[keys, :])
        car = car + full[:, SB_T:]
        car_scr[c] = car
        return still_live(car, kb - 1)

    def more(state):
        live = state[1]
        for flag in state[2:]:
            live = live | flag
        return live > 0

    def walk(state):
        d = state[0]
        flags = [lax.cond(state[1 + c] > 0, functools.partial(one_more_tile, c, d), lambda: jnp.int32(0))
                 for c in range(n_sub)]
        return (d + 1, *flags)

    lax.while_loop(more, walk, (jnp.int32(2), *lives))

    for c in range(n_sub):
        o_ref[c * SB_T:(c + 1) * SB_T, :] = jnp.where(
            lo_mask, acc_scr[c, :SB_T, :], acc_scr[c, SB_T:, :]).astype(o_ref.dtype)


def _sb_attn(proj, q_gain, k_gain, batch, seq):
    t = proj.shape[0]
    hpg = D_MODEL // LANES
    n_pairs = SB_HEADS // 2
    n_qt = seq // SB_QROWS
    n_sub = SB_QROWS // SB_T
    return pl.pallas_call(
        _sb_kernel,
        grid=(batch, n_pairs, n_qt),
        in_specs=[
            pl.BlockSpec((SB_QROWS, LANES), lambda b, p, i: (b * n_qt + i, P_SQ * hpg + p)),
            pl.BlockSpec((seq, LANES), lambda b, p, i: (b, P_SK * hpg + p)),
            pl.BlockSpec((seq, LANES), lambda b, p, i: (b, P_SV * hpg + p)),
            pl.BlockSpec((None, 1, LANES), lambda b, p, i: (p, 0, 0)),
            pl.BlockSpec((None, 1, LANES), lambda b, p, i: (p, 0, 0)),
        ],
        out_specs=pl.BlockSpec((SB_QROWS, LANES), lambda b, p, i: (b * n_qt + i, p)),
        out_shape=jax.ShapeDtypeStruct((t, SB_HEADS * SB_DH), bf16),
        scratch_shapes=[
            pltpu.VMEM((seq, LANES), bf16),
            pltpu.VMEM((n_sub, 2 * SB_T, LANES), bf16),
            pltpu.VMEM((n_sub, 2 * SB_T, LANES), f32),
            pltpu.VMEM((n_sub, 2 * SB_T, LANES), f32),
        ],
        compiler_params=pltpu.CompilerParams(
            dimension_semantics=("arbitrary", "arbitrary", "arbitrary"), vmem_limit_bytes=VMEM_LIMIT),
        name="sb_attn",
    )(proj, proj, proj, q_gain.reshape(n_pairs, 1, LANES), k_gain.reshape(n_pairs, 1, LANES))


def _merge_kernel(x_ref, ohg_ref, osb_ref, ga_ref, gb_ref, whg_ref, wsb_ref, wo_ref, o_ref):
    half = x_ref.shape[0] // 2
    parts = [slice(0, half), slice(half, 2 * half)]
    y_hg = [_dot(ohg_ref[r, :], whg_ref[...]) for r in parts]
    y_sb = [_dot(osb_ref[r, :], wsb_ref[...]) for r in parts]
    for r, yh, ys in zip(parts, y_hg, y_sb):
        mixed = ga_ref[r, :].astype(f32) * yh + gb_ref[r, :].astype(f32) * ys
        o_ref[r, :] = x_ref[r, :] + _dot(mixed.astype(bf16), wo_ref[...])


def _merge(x2, o_hg, o_sb, proj, w_hg, w_sb, w_o, tm=1024):
    t = x2.shape[0]
    rows = lambda i: (i, 0)
    resident = dict(pipeline_mode=pl.Buffered(1))
    return pl.pallas_call(
        _merge_kernel,
        grid=(t // tm,),
        in_specs=[
            pl.BlockSpec((tm, D_MODEL), rows),
            pl.BlockSpec((tm, D_MODEL), rows),
            pl.BlockSpec((tm, D_MODEL), rows),
            pl.BlockSpec((tm, D_MODEL), lambda i: (i, P_GA)),
            pl.BlockSpec((tm, D_MODEL), lambda i: (i, P_GB)),
            pl.BlockSpec((D_MODEL, D_MODEL), lambda i: (0, 0), **resident),
            pl.BlockSpec((D_MODEL, D_MODEL), lambda i: (0, 0), **resident),
            pl.BlockSpec((D_MODEL, D_MODEL), lambda i: (0, 0), **resident),
        ],
        out_specs=pl.BlockSpec((tm, D_MODEL), rows),
        out_shape=jax.ShapeDtypeStruct((t, D_MODEL), f32),
        compiler_params=pltpu.CompilerParams(
            dimension_semantics=("arbitrary",), vmem_limit_bytes=VMEM_LIMIT),
        name="merge",
    )(x2, o_hg, o_sb, proj, proj, w_hg, w_sb, w_o)


def _ffn_kernel(x_ref, gain_ref, wi_ref, wd_ref, o_ref, act_scr):
    x = x_ref[...]
    ms = jnp.mean(x * x, axis=-1, keepdims=True)
    h = (x * lax.rsqrt(ms + EPS) * gain_ref[...]).astype(bf16)
    for j in range(FF_HIDDEN // FF_CHUNK):
        cols = slice(j * FF_CHUNK, (j + 1) * FF_CHUNK)
        up_cols = slice(FF_HIDDEN + j * FF_CHUNK, FF_HIDDEN + (j + 1) * FF_CHUNK)
        gate = _dot(h, wi_ref[:, cols])
        up = _dot(h, wi_ref[:, up_cols])
        act_scr[:, cols] = (gate * _sigmoid(gate) * up).astype(bf16)
    o_ref[...] = x + _dot(act_scr[...], wd_ref[...])


def _ffn(x1, gain, w_in_bf, w_out_bf, tm=1024):
    t = x1.shape[0]
    resident = dict(pipeline_mode=pl.Buffered(1))
    return pl.pallas_call(
        _ffn_kernel,
        grid=(t // tm,),
        in_specs=[
            pl.BlockSpec((tm, D_MODEL), lambda i: (i, 0)),
            pl.BlockSpec((1, D_MODEL), lambda i: (0, 0)),
            pl.BlockSpec((D_MODEL, 2 * FF_HIDDEN), lambda i: (0, 0), **resident),
            pl.BlockSpec((FF_HIDDEN, D_MODEL), lambda i: (0, 0), **resident),
        ],
        out_specs=pl.BlockSpec((tm, D_MODEL), lambda i: (i, 0)),
        out_shape=jax.ShapeDtypeStruct((t, D_MODEL), f32),
        scratch_shapes=[pltpu.VMEM((tm, FF_HIDDEN), bf16)],
        compiler_params=pltpu.CompilerParams(
            dimension_semantics=("arbitrary",), vmem_limit_bytes=VMEM_LIMIT),
        name="ffn",
    )(x1, gain, w_in_bf, w_out_bf)


def kernel(x, norm1_gain, w_in, lb_logits, hg_out_norm, sb_q_norm, sb_k_norm,
           w_hg_out, w_sb_out, w_o, norm2_gain, w_ffn_in, w_ffn_out):
    batch, seq, d = x.shape
    depth = norm1_gain.shape[0]
    x2 = x.reshape(batch * seq, d)
    for layer in range(depth):
        gain1 = norm1_gain[layer][None, :]
        proj, h = _in_proj(x2, gain1, w_in[layer])
        o_hg = _hgrn2(proj, h, w_in[layer], lb_logits, hg_out_norm[layer], layer, batch, seq)
        o_sb = _sb_attn(proj, sb_q_norm[layer], sb_k_norm[layer], batch, seq)
        x1 = _merge(x2, o_hg, o_sb, proj, w_hg_out[layer].astype(bf16),
                    w_sb_out[layer].astype(bf16), w_o[layer].astype(bf16))
        x2 = _ffn(x1, norm2_gain[layer][None, :], w_ffn_in[layer].astype(bf16),
                  w_ffn_out[layer].astype(bf16))
    return x2.reshape(batch, seq, d)
```

```python
import functools
import math

import jax
import jax.numpy as jnp
from jax import lax
from jax.experimental import pallas as pl
from jax.experimental.pallas import tpu as pltpu

D_MODEL = 1024
HG_HEADS = 8
HG_DK = 128
HG_DV = 128
SB_HEADS = 16
SB_DH = 64
FF_HIDDEN = 2816
FF_CHUNK = 256
EPS = 1e-6
G_HQ, G_HF, G_HI, G_HG, G_SQ, G_SK, G_SV, G_GA, G_GB = range(9)
P_SOURCE = (G_HQ, G_HI, G_HG, G_SQ, G_SK, G_SV, G_GA, G_GB)
P_HQ, P_HI, P_HG, P_SQ, P_SK, P_SV, P_GA, P_GB = range(8)

LANES = 128
HG_CHUNK = 64
HG_PAR = 4
HG_ROWS = 2048
HG_EXP_CLAMP = 80.0
SB_T = 128
SB_QROWS = 2048
SB_DONE_LOG2 = 128.0
VMEM_LIMIT = 56 * 1024 * 1024

f32 = jnp.float32
bf16 = jnp.bfloat16
NEG_INF = float("-inf")


def _sigmoid(x):
    return 1.0 / (1.0 + jnp.exp(-x))


def _sigmoid_tanh(x):
    return 0.5 * jnp.tanh(0.5 * x) + 0.5


def _dot(a, b):
    return jnp.dot(a, b, preferred_element_type=f32)


def _dot_nt(a, b):
    return lax.dot_general(a, b, (((1,), (1,)), ((), ())), preferred_element_type=f32)


def _dot_tn(a, b):
    return lax.dot_general(a, b, (((0,), (0,)), ((), ())), preferred_element_type=f32)


def _split2(x):
    hi = x.astype(bf16)
    lo = (x - hi.astype(f32)).astype(bf16)
    return hi, lo


def _rms_norm_bf16(x, gain):
    ms = jnp.mean(x * x, axis=-1, keepdims=True)
    return (x * lax.rsqrt(ms + EPS) * gain).astype(bf16)


def _inproj_kernel(x_ref, gain_ref, w_ref, o_ref, h_ref):
    j = pl.program_id(1)

    @pl.when(j == 0)
    def _():
        w = w_ref[...].astype(bf16)
        n_blk = 4
        blk = x_ref.shape[0] // n_blk
        for b in range(n_blk):
            rows = slice(b * blk, (b + 1) * blk)
            h = _rms_norm_bf16(x_ref[rows, :], gain_ref[...])
            h_ref[rows, :] = h
            o_ref[rows, :] = _dot(h, w).astype(o_ref.dtype)

    is_gate = (j == P_HG) | (j == P_GA) | (j == P_GB)

    @pl.when(is_gate)
    def _():
        o_ref[...] = _sigmoid_tanh(_dot(h_ref[...], w_ref[...].astype(bf16))).astype(o_ref.dtype)

    @pl.when(jnp.logical_not(is_gate) & (j > 0))
    def _():
        o_ref[...] = _dot(h_ref[...], w_ref[...].astype(bf16)).astype(o_ref.dtype)


def _in_proj(x2, gain, w_in, tm=2048):
    t = x2.shape[0]
    n_groups = len(P_SOURCE)
    return pl.pallas_call(
        _inproj_kernel,
        grid=(t // tm, n_groups),
        in_specs=[
            pl.BlockSpec((tm, D_MODEL), lambda i, j: (i, 0)),
            pl.BlockSpec((1, D_MODEL), lambda i, j: (0, 0)),
            pl.BlockSpec((D_MODEL, D_MODEL), lambda i, j: (0, jnp.where(j >= G_HF, j + 1, j))),
        ],
        out_specs=[
            pl.BlockSpec((tm, D_MODEL), lambda i, j: (i, j)),
            pl.BlockSpec((tm, D_MODEL), lambda i, j: (i, 0)),
        ],
        out_shape=[
            jax.ShapeDtypeStruct((t, n_groups * D_MODEL), bf16),
            jax.ShapeDtypeStruct((t, D_MODEL), bf16),
        ],
        compiler_params=pltpu.CompilerParams(
            dimension_semantics=("arbitrary", "arbitrary"), vmem_limit_bytes=VMEM_LIMIT),
        name="in_proj",
    )(x2, gain, w_in)


def _hgrn_kernel(q_ref, h_ref, i_ref, og_ref, w_ref, lbl_ref, gain_ref, o_ref, st_scr, w_scr, fg_scr,
                 *, layer):
    c_len = HG_CHUNK
    blk = HG_PAR * c_len
    n_blk = q_ref.shape[0] // blk

    @pl.when((pl.program_id(0) == 0) & (pl.program_id(1) == 0))
    def _():
        w_scr[...] = w_ref[...].astype(bf16)

    @pl.when(pl.program_id(1) == 0)
    def _():
        st_scr[...] = jnp.zeros_like(st_scr)

    l = lbl_ref[...]
    e = jnp.exp(l - jnp.max(l, axis=0, keepdims=True))
    lb = jnp.sum(e[0:layer + 1, :], axis=0, keepdims=True) / jnp.sum(e, axis=0, keepdims=True)
    f_mid = 0.5 * (1.0 + lb)
    f_amp = 0.5 * (1.0 - lb)
    gain = gain_ref[...]

    def forget_block(it, slot):
        rows = pl.ds(pl.multiple_of(it * blk, blk), blk)
        f = f_mid + f_amp * jnp.tanh(0.5 * _dot(h_ref[rows, :], w_scr[...]))
        fg_scr[slot, 0], fg_scr[slot, 1] = _split2(jnp.log(f))
        fg_scr[slot, 2] = (1.0 - f).astype(bf16)

    row = lax.broadcasted_iota(jnp.int32, (c_len, c_len), 0)
    col = lax.broadcasted_iota(jnp.int32, (c_len, c_len), 1)
    causal = row >= col
    tri = jnp.where(causal, 1.0, 0.0).astype(bf16)
    tri2 = jnp.concatenate([tri, tri], axis=1)
    dv_mean = jnp.full((HG_DV, HG_DV), 1.0 / HG_DV, bf16)
    heads = range(HG_HEADS)
    sl = [slice(h * HG_DK, (h + 1) * HG_DK) for h in heads]

    def chunk_operands(it, p, slot):
        rows = pl.ds(pl.multiple_of(it * blk + p * c_len, c_len), c_len)
        in_blk = slice(p * c_len, (p + 1) * c_len)
        cum = _dot(tri2, jnp.concatenate([fg_scr[slot, 0, in_blk, :], fg_scr[slot, 1, in_blk, :]], axis=0))
        k = fg_scr[slot, 2, in_blk, :].astype(f32)
        last = cum[c_len - 1:c_len, :]
        mid = cum[c_len // 2 - 1:c_len // 2, :]
        q = q_ref[rows, :].astype(f32)
        return dict(
            rows=rows,
            qt=(q * jnp.exp(jnp.minimum(cum - mid, HG_EXP_CLAMP))).astype(bf16),
            kt=(k * jnp.exp(jnp.minimum(mid - cum, HG_EXP_CLAMP))).astype(bf16),
            qd=(q * jnp.exp(cum)).astype(bf16),
            kd=(k * jnp.exp(last - cum)).astype(bf16),
            e_last=jnp.exp(last),
            v=i_ref[rows, :],
            out_scale=gain * og_ref[rows, :].astype(f32),
        )

    def body(it, carry):
        slot = it % 2
        cs = [chunk_operands(it, p, slot) for p in range(HG_PAR)]
        forget_block(jnp.minimum(it + 1, n_blk - 1), 1 - slot)
        scores =[[jnp.where(causal, _dot_nt(c["qt"][:, sl[h]], c["kt"][:, sl[h]]), 0.0).astype(bf16)
                   for h in heads] for c in cs]
        upd = [[_dot_tn(c["v"][:, sl[h]], c["kd"][:, sl[h]]) for h in heads] for c in cs]
        states = [st_scr[h] for h in heads]
        inter = []
        for p, c in enumerate(cs):
            inter.append([_dot_nt(c["qd"][:, sl[h]], states[h].astype(bf16)) for h in heads])
            states = [states[h] * c["e_last"][:, sl[h]] + upd[p][h] for h in heads]
        for h in heads:
            st_scr[h] = states[h]
        o = [[_dot(scores[p][h], c["v"][:, sl[h]]) + inter[p][h] for h in heads] for p, c in enumerate(cs)]
        ms = [[_dot((o[p][h] * o[p][h]).astype(bf16), dv_mean) for h in heads] for p in range(HG_PAR)]
        for p, c in enumerate(cs):
            outs = [(o[p][h] * lax.rsqrt(ms[p][h] + EPS) * c["out_scale"][:, sl[h]]).astype(o_ref.dtype)
                    for h in heads]
            o_ref[c["rows"], :] = jnp.concatenate(outs, axis=1)
        return carry

    forget_block(0, 0)
    lax.fori_loop(0, n_blk, body, 0)


def _hgrn2(proj, h, w_in, lb_logits, hg_gain, layer, batch, seq):
    t = proj.shape[0]
    n_rt = seq // HG_ROWS

    def grp(group):
        return pl.BlockSpec((HG_ROWS, D_MODEL), lambda b, r: (b * n_rt + r, group))

    return pl.pallas_call(
        functools.partial(_hgrn_kernel, layer=layer),
        grid=(batch, n_rt),
        in_specs=[
            grp(P_HQ), grp(0), grp(P_HI), grp(P_HG),
            pl.BlockSpec((D_MODEL, D_MODEL), lambda b, r: (0, G_HF), pipeline_mode=pl.Buffered(1)),
            pl.BlockSpec(lb_logits.shape, lambda b, r: (0, 0)),
            pl.BlockSpec((1, D_MODEL), lambda b, r: (0, 0)),
        ],
        out_specs=grp(0),
        out_shape=jax.ShapeDtypeStruct((t, HG_HEADS * HG_DV), bf16),
        scratch_shapes=[
            pltpu.VMEM((HG_HEADS, HG_DV, HG_DK), f32),
            pltpu.VMEM((D_MODEL, D_MODEL), bf16),
            pltpu.VMEM((2, 3, HG_PAR * HG_CHUNK, D_MODEL), bf16),
        ],
        compiler_params=pltpu.CompilerParams(
            dimension_semantics=("arbitrary", "arbitrary"), vmem_limit_bytes=VMEM_LIMIT),
        name="hgrn2",
    )(proj, h, proj, proj, w_in, lb_logits, hg_gain.reshape(1, HG_HEADS * HG_DV))


def _pair_rms(blocks, gain, group_mean):
    ms = [_dot((x * x).astype(bf16), group_mean) for x in blocks]
    return [x * lax.rsqrt(m + EPS) * gain for x, m in zip(blocks, ms)]


def _softplus2(z):
    return jnp.maximum(z, 0.0) + jnp.log2(1.0 + jnp.exp2(-jnp.abs(z)))


def _sb_kernel(q_ref, k_ref, v_ref, qg_ref, kg_ref, o_ref, kn_scr, q2_scr, acc_scr, car_scr):
    step = pl.program_id(2)
    seq = k_ref.shape[0]
    n_sub = SB_QROWS // SB_T
    lane = lax.broadcasted_iota(jnp.int32, (1, LANES), 1)
    lo_mask = lane < SB_DH

    gr = lax.broadcasted_iota(jnp.int32, (LANES, LANES), 0) // SB_DH
    gc = lax.broadcasted_iota(jnp.int32, (LANES, LANES), 1) // SB_DH
    group_mean = jnp.where(gr == gc, 1.0 / SB_DH, 0.0).astype(bf16)

    @pl.when(step == 0)
    def _():
        prep = 512

        blocks = [k_ref[r * prep:(r + 1) * prep, :].astype(f32) for r in range(seq // prep)]
        for r, kn in enumerate(_pair_rms(blocks, kg_ref[...], group_mean)):
            kn_scr[r * prep:(r + 1) * prep, :] = kn.astype(bf16)

    q_scale = math.log2(math.e) / math.sqrt(SB_DH)
    qn = _pair_rms([q_ref[c * SB_T:(c + 1) * SB_T, :].astype(f32) for c in range(n_sub)],
                   qg_ref[...] * q_scale, group_mean)
    q2 = [jnp.concatenate([jnp.where(lo_mask, q, 0.0).astype(bf16),
                           jnp.where(lo_mask, 0.0, q).astype(bf16)], axis=0) for q in qn]

    row = lax.broadcasted_iota(jnp.int32, (2 * SB_T, SB_T), 0) % SB_T
    col = lax.broadcasted_iota(jnp.int32, (2 * SB_T, SB_T), 1)
    earlier = col < row
    urow = lax.broadcasted_iota(jnp.int32, (2 * SB_T, 2 * SB_T), 0)
    ucol = lax.broadcasted_iota(jnp.int32, (2 * SB_T, 2 * SB_T), 1)
    suffix2 = jnp.where((urow >= ucol) | (ucol >= SB_T), 1.0, 0.0).astype(bf16)[:SB_T]
    suffix_pair = jnp.where((urow >= ucol) & ((urow < SB_T) == (ucol < SB_T)) | (urow >= SB_T) & (ucol < SB_T),
                            1.0, 0.0).astype(bf16)

    first = step == 0
    k_start = [jnp.where(first, 0, step * n_sub - 1) * SB_T] + [
        (step * n_sub + c - 1) * SB_T for c in range(1, n_sub)]
    pair_keys = [pl.ds(pl.multiple_of(k0, SB_T), 2 * SB_T) for k0 in k_start]

    zs = [_dot_nt(q2[c], kn_scr[pair_keys[c], :]) for c in range(n_sub)]
    off = jnp.where(first, 0, SB_T)
    zl = [zs[c][:, :SB_T] for c in range(n_sub)]
    zd = [jnp.where(col + SB_T < row + off if c == 0 else earlier, zs[c][:, SB_T:], NEG_INF)
          for c in range(n_sub)]
    zl[0] = jnp.where(col < row + off, zl[0], NEG_INF)
    zs = [jnp.concatenate([zl[c], zd[c]], axis=1) for c in range(n_sub)]
    rs = [_dot(_softplus2(zs[c]).astype(bf16), suffix_pair) for c in range(n_sub)]

    def still_live(car, next_tile):
        m = jnp.min(jnp.min(car, axis=1, keepdims=True), axis=0, keepdims=True)
        return ((next_tile >= 0) & (m[0, 0] <= SB_DONE_LOG2)).astype(jnp.int32)

    accs, cars, lives = [], [], []
    for c in range(n_sub):
        a = jnp.exp2(zs[c] - rs[c])
        accs.append(_dot(a.astype(bf16), v_ref[pair_keys[c], :]))
        total = rs[c][:, 0:1]
        cars.append(jnp.broadcast_to(total, (2 * SB_T, SB_T)))
        lives.append(still_live(total, step * n_sub + c - 2))
    for c in range(n_sub):
        q2_scr[c] = q2[c]
        acc_scr[c] = accs[c]
        car_scr[c] = cars[c]

    def one_more_tile(c, d):
        kb = step * n_sub + c - d
        keys = pl.ds(pl.multiple_of(kb * SB_T, SB_T), SB_T)
        z = _dot_nt(q2_scr[c], kn_scr[keys, :])
        full = _dot(_softplus2(z).astype(bf16), suffix2)
        car = car_scr[c]
        a = jnp.exp2(z - (full[:, :SB_T] + car))
        acc_scr[c] += _dot(a.astype(bf16), v_ref[keys, :])
        car = car + full[:, SB_T:]
        car_scr[c] = car
        return still_live(car, kb - 1)

    def more(state):
        live = state[1]
        for flag in state[2:]:
            live = live | flag
        return live > 0

    def walk(state):
        d = state[0]
        flags = [lax.cond(state[1 + c] > 0, functools.partial(one_more_tile, c, d), lambda: jnp.int32(0))
                 for c in range(n_sub)]
        return (d + 1, *flags)

    lax.while_loop(more, walk, (jnp.int32(2), *lives))

    for c in range(n_sub):
        o_ref[c * SB_T:(c + 1) * SB_T, :] = jnp.where(
            lo_mask, acc_scr[c, :SB_T, :], acc_scr[c, SB_T:, :]).astype(o_ref.dtype)


def _sb_attn(proj, q_gain, k_gain, batch, seq):
    t = proj.shape[0]
    hpg = D_MODEL // LANES
    n_pairs = SB_HEADS // 2
    n_qt = seq // SB_QROWS
    n_sub = SB_QROWS // SB_T
    return pl.pallas_call(
        _sb_kernel,
        grid=(batch, n_pairs, n_qt),
        in_specs=[
            pl.BlockSpec((SB_QROWS, LANES), lambda b, p, i: (b * n_qt + i, P_SQ * hpg + p)),
            pl.BlockSpec((seq, LANES), lambda b, p, i: (b, P_SK * hpg + p)),
            pl.BlockSpec((seq, LANES), lambda b, p, i: (b, P_SV * hpg + p)),
            pl.BlockSpec((None, 1, LANES), lambda b, p, i: (p, 0, 0)),
            pl.BlockSpec((None, 1, LANES), lambda b, p, i: (p, 0, 0)),
        ],
        out_specs=pl.BlockSpec((SB_QROWS, LANES), lambda b, p, i: (b * n_qt + i, p)),
        out_shape=jax.ShapeDtypeStruct((t, SB_HEADS * SB_DH), bf16),
        scratch_shapes=[
            pltpu.VMEM((seq, LANES), bf16),
            pltpu.VMEM((n_sub, 2 * SB_T, LANES), bf16),
            pltpu.VMEM((n_sub, 2 * SB_T, LANES), f32),
            pltpu.VMEM((n_sub, 2 * SB_T, LANES), f32),
        ],
        compiler_params=pltpu.CompilerParams(
            dimension_semantics=("arbitrary", "arbitrary", "arbitrary"), vmem_limit_bytes=VMEM_LIMIT),
        name="sb_attn",
    )(proj, proj, proj, q_gain.reshape(n_pairs, 1, LANES), k_gain.reshape(n_pairs, 1, LANES))


def _merge_kernel(x_ref, ohg_ref, osb_ref, ga_ref, gb_ref, whg_ref, wsb_ref, wo_ref, o_ref):
    half = x_ref.shape[0] // 2
    parts = [slice(0, half), slice(half, 2 * half)]
    y_hg = [_dot(ohg_ref[r, :], whg_ref[...]) for r in parts]
    y_sb = [_dot(osb_ref[r, :], wsb_ref[...]) for r in parts]
    for r, yh, ys in zip(parts, y_hg, y_sb):
        mixed = ga_ref[r, :].astype(f32) * yh + gb_ref[r, :].astype(f32) * ys
        o_ref[r, :] = x_ref[r, :] + _dot(mixed.astype(bf16), wo_ref[...])


def _merge(x2, o_hg, o_sb, proj, w_hg, w_sb, w_o, tm=1024):
    t = x2.shape[0]
    rows = lambda i: (i, 0)
    resident = dict(pipeline_mode=pl.Buffered(1))
    return pl.pallas_call(
        _merge_kernel,
        grid=(t // tm,),
        in_specs=[
            pl.BlockSpec((tm, D_MODEL), rows),
            pl.BlockSpec((tm, D_MODEL), rows),
            pl.BlockSpec((tm, D_MODEL), rows),
            pl.BlockSpec((tm, D_MODEL), lambda i: (i, P_GA)),
            pl.BlockSpec((tm, D_MODEL), lambda i: (i, P_GB)),
            pl.BlockSpec((D_MODEL, D_MODEL), lambda i: (0, 0), **resident),
            pl.BlockSpec((D_MODEL, D_MODEL), lambda i: (0, 0), **resident),
            pl.BlockSpec((D_MODEL, D_MODEL), lambda i: (0, 0), **resident),
        ],
        out_specs=pl.BlockSpec((tm, D_MODEL), rows),
        out_shape=jax.ShapeDtypeStruct((t, D_MODEL), f32),
        compiler_params=pltpu.CompilerParams(
            dimension_semantics=("arbitrary",), vmem_limit_bytes=VMEM_LIMIT),
        name="merge",
    )(x2, o_hg, o_sb, proj, proj, w_hg, w_sb, w_o)


def _ffn_kernel(x_ref, gain_ref, wi_ref, wd_ref, o_ref, act_scr):
    x = x_ref[...]
    ms = jnp.mean(x * x, axis=-1, keepdims=True)
    h = (x * lax.rsqrt(ms + EPS) * gain_ref[...]).astype(bf16)
    for j in range(FF_HIDDEN // FF_CHUNK):
        cols = slice(j * FF_CHUNK, (j + 1) * FF_CHUNK)
        up_cols = slice(FF_HIDDEN + j * FF_CHUNK, FF_HIDDEN + (j + 1) * FF_CHUNK)
        gate = _dot(h, wi_ref[:, cols])
        up = _dot(h, wi_ref[:, up_cols])
        act_scr[:, cols] = (gate * _sigmoid(gate) * up).astype(bf16)
    o_ref[...] = x + _dot(act_scr[...], wd_ref[...])


def _ffn(x1, gain, w_in_bf, w_out_bf, tm=1024):
    t = x1.shape[0]
    resident = dict(pipeline_mode=pl.Buffered(1))
    return pl.pallas_call(
        _ffn_kernel,
        grid=(t // tm,),
        in_specs=[
            pl.BlockSpec((tm, D_MODEL), lambda i: (i, 0)),
            pl.BlockSpec((1, D_MODEL), lambda i: (0, 0)),
            pl.BlockSpec((D_MODEL, 2 * FF_HIDDEN), lambda i: (0, 0), **resident),
            pl.BlockSpec((FF_HIDDEN, D_MODEL), lambda i: (0, 0), **resident),
        ],
        out_specs=pl.BlockSpec((tm, D_MODEL), lambda i: (i, 0)),
        out_shape=jax.ShapeDtypeStruct((t, D_MODEL), f32),
        scratch_shapes=[pltpu.VMEM((tm, FF_HIDDEN), bf16)],
        compiler_params=pltpu.CompilerParams(
            dimension_semantics=("arbitrary",), vmem_limit_bytes=VMEM_LIMIT),
        name="ffn",
    )(x1, gain, w_in_bf, w_out_bf)


def kernel(x, norm1_gain, w_in, lb_logits, hg_out_norm, sb_q_norm, sb_k_norm,
           w_hg_out, w_sb_out, w_o, norm2_gain, w_ffn_in, w_ffn_out):
    batch, seq, d = x.shape
    depth = norm1_gain.shape[0]
    x2 = x.reshape(batch * seq, d)
    for layer in range(depth):
        gain1 = norm1_gain[layer][None, :]
        proj, h = _in_proj(x2, gain1, w_in[layer])
        o_hg = _hgrn2(proj, h, w_in[layer], lb_logits, hg_out_norm[layer], layer, batch, seq)
        o_sb = _sb_attn(proj, sb_q_norm[layer], sb_k_norm[layer], batch, seq)
        x1 = _merge(x2, o_hg, o_sb, proj, w_hg_out[layer].astype(bf16),
                    w_sb_out[layer].astype(bf16), w_o[layer].astype(bf16))
        x2 = _ffn(x1, norm2_gain[layer][None, :], w_ffn_in[layer].astype(bf16),
                  w_ffn_out[layer].astype(bf16))
    return x2.reshape(batch, seq, d)
```

```python
import functools
import math

import jax
import jax.numpy as jnp
from jax import lax
from jax.experimental import pallas as pl
from jax.experimental.pallas import tpu as pltpu

D_MODEL = 1024
HG_HEADS = 8
HG_DK = 128
HG_DV = 128
SB_HEADS = 16
SB_DH = 64
FF_HIDDEN = 2816
FF_CHUNK = 256
EPS = 1e-6
G_HQ, G_HF, G_HI, G_HG, G_SQ, G_SK, G_SV, G_GA, G_GB = range(9)
P_SOURCE = (G_HQ, G_HI, G_HG, G_SQ, G_SK, G_SV, G_GA, G_GB)
P_HQ, P_HI, P_HG, P_SQ, P_SK, P_SV, P_GA, P_GB = range(8)

LANES = 128
HG_CHUNK = 64
HG_PAR = 4
HG_ROWS = 2048
HG_EXP_CLAMP = 80.0
SB_T = 128
SB_QROWS = 2048
SB_DONE_LOG2 = 128.0
VMEM_LIMIT = 56 * 1024 * 1024

f32 = jnp.float32
bf16 = jnp.bfloat16
NEG_INF = float("-inf")


def _sigmoid(x):
    return 1.0 / (1.0 + jnp.exp(-x))


def _sigmoid_tanh(x):
    return 0.5 * jnp.tanh(0.5 * x) + 0.5


def _dot(a, b):
    return jnp.dot(a, b, preferred_element_type=f32)


def _dot_nt(a, b):
    return lax.dot_general(a, b, (((1,), (1,)), ((), ())), preferred_element_type=f32)


def _dot_tn(a, b):
    return lax.dot_general(a, b, (((0,), (0,)), ((), ())), preferred_element_type=f32)


def _split2(x):
    hi = x.astype(bf16)
    lo = (x - hi.astype(f32)).astype(bf16)
    return hi, lo


def _rms_norm_bf16(x, gain):
    ms = jnp.mean(x * x, axis=-1, keepdims=True)
    return (x * lax.rsqrt(ms + EPS) * gain).astype(bf16)


def _inproj_kernel(x_ref, gain_ref, w_ref, o_ref, h_ref):
    j = pl.program_id(1)

    @pl.when(j == 0)
    def _():
        w = w_ref[...].astype(bf16)
        n_blk = 4
        blk = x_ref.shape[0] // n_blk
        for b in range(n_blk):
            rows = slice(b * blk, (b + 1) * blk)
            h = _rms_norm_bf16(x_ref[rows, :], gain_ref[...])
            h_ref[rows, :] = h
            o_ref[rows, :] = _dot(h, w).astype(o_ref.dtype)

    is_gate = j == P_HG

    @pl.when(is_gate)
    def _():
        o_ref[...] = _sigmoid_tanh(_dot(h_ref[...], w_ref[...].astype(bf16))).astype(o_ref.dtype)

    @pl.when(jnp.logical_not(is_gate) & (j > 0))
    def _():
        o_ref[...] = _dot(h_ref[...], w_ref[...].astype(bf16)).astype(o_ref.dtype)


def _in_proj(x2, gain, w_in, tm=2048):
    t = x2.shape[0]
    n_groups = len(P_SOURCE)
    return pl.pallas_call(
        _inproj_kernel,
        grid=(t // tm, n_groups),
        in_specs=[
            pl.BlockSpec((tm, D_MODEL), lambda i, j: (i, 0)),
            pl.BlockSpec((1, D_MODEL), lambda i, j: (0, 0)),
            pl.BlockSpec((D_MODEL, D_MODEL), lambda i, j: (0, jnp.where(j >= G_HF, j + 1, j))),
        ],
        out_specs=[
            pl.BlockSpec((tm, D_MODEL), lambda i, j: (i, j)),
            pl.BlockSpec((tm, D_MODEL), lambda i, j: (i, 0)),
        ],
        out_shape=[
            jax.ShapeDtypeStruct((t, n_groups * D_MODEL), bf16),
            jax.ShapeDtypeStruct((t, D_MODEL), bf16),
        ],
        compiler_params=pltpu.CompilerParams(
            dimension_semantics=("arbitrary", "arbitrary"), vmem_limit_bytes=VMEM_LIMIT),
        name="in_proj",
    )(x2, gain, w_in)


def _hgrn_kernel(q_ref, h_ref, i_ref, og_ref, w_ref, lbl_ref, gain_ref, o_ref, st_scr, w_scr, fg_scr,
                 *, layer):
    c_len = HG_CHUNK
    blk = HG_PAR * c_len
    n_blk = q_ref.shape[0] // blk

    @pl.when((pl.program_id(0) == 0) & (pl.program_id(1) == 0))
    def _():
        w_scr[...] = w_ref[...].astype(bf16)

    @pl.when(pl.program_id(1) == 0)
    def _():
        st_scr[...] = jnp.zeros_like(st_scr)

    l = lbl_ref[...]
    e = jnp.exp(l - jnp.max(l, axis=0, keepdims=True))
    lb = jnp.sum(e[0:layer + 1, :], axis=0, keepdims=True) / jnp.sum(e, axis=0, keepdims=True)
    f_mid = 0.5 * (1.0 + lb)
    f_amp = 0.5 * (1.0 - lb)
    gain = gain_ref[...]

    def forget_block(it, slot):
        rows = pl.ds(pl.multiple_of(it * blk, blk), blk)
        f = f_mid + f_amp * jnp.tanh(0.5 * _dot(h_ref[rows, :], w_scr[...]))
        fg_scr[slot, 0], fg_scr[slot, 1] = _split2(jnp.log(f))
        fg_scr[slot, 2] = (1.0 - f).astype(bf16)

    row = lax.broadcasted_iota(jnp.int32, (c_len, c_len), 0)
    col = lax.broadcasted_iota(jnp.int32, (c_len, c_len), 1)
    causal = row >= col
    tri = jnp.where(causal, 1.0, 0.0).astype(bf16)
    tri2 = jnp.concatenate([tri, tri], axis=1)
    heads = range(HG_HEADS)
    sl = [slice(h * HG_DK, (h + 1) * HG_DK) for h in heads]

    def chunk_operands(it, p, slot):
        rows = pl.ds(pl.multiple_of(it * blk + p * c_len, c_len), c_len)
        in_blk = slice(p * c_len, (p + 1) * c_len)
        cum = _dot(tri2, jnp.concatenate([fg_scr[slot, 0, in_blk, :], fg_scr[slot, 1, in_blk, :]], axis=0))
        k = fg_scr[slot, 2, in_blk, :].astype(f32)
        last = cum[c_len - 1:c_len, :]
        mid = 0.5 * last
        q = q_ref[rows, :].astype(f32)
        return dict(
            rows=rows,
            qt=(q * jnp.exp(jnp.minimum(cum - mid, HG_EXP_CLAMP))).astype(bf16),
            kt=(k * jnp.exp(jnp.minimum(mid - cum, HG_EXP_CLAMP))).astype(bf16),
            qd=(q * jnp.exp(cum)).astype(bf16),
            kd=(k * jnp.exp(last - cum)).astype(bf16),
            e_last=jnp.exp(last),
            v=i_ref[rows, :],
            out_scale=gain * og_ref[rows, :].astype(f32),
        )

    def body(it, carry):
        slot = it % 2
        cs = [chunk_operands(it, p, slot) for p in range(HG_PAR)]
        forget_block(jnp.minimum(it + 1, n_blk - 1), 1 - slot)
        scores =[[jnp.where(causal, _dot_nt(c["qt"][:, sl[h]], c["kt"][:, sl[h]]), 0.0).astype(bf16)
                   for h in heads] for c in cs]
        upd = [[_dot_tn(c["v"][:, sl[h]], c["kd"][:, sl[h]]) for h in heads] for c in cs]
        states = [st_scr[h] for h in heads]
        inter = []
        for p, c in enumerate(cs):
            inter.append([_dot_nt(c["qd"][:, sl[h]], states[h].astype(bf16)) for h in heads])
            states = [states[h] * c["e_last"][:, sl[h]] + upd[p][h] for h in heads]
        for h in heads:
            st_scr[h] = states[h]
        for p, c in enumerate(cs):
            outs = []
            for h in heads:
                o = _dot(scores[p][h], c["v"][:, sl[h]]) + inter[p][h]
                ms = jnp.mean(o * o, axis=-1, keepdims=True)
                outs.append((o * lax.rsqrt(ms + EPS) * c["out_scale"][:, sl[h]]).astype(o_ref.dtype))
            o_ref[c["rows"], :] = jnp.concatenate(outs, axis=1)
        return carry

    forget_block(0, 0)
    lax.fori_loop(0, n_blk, body, 0)


def _hgrn2(proj, h, w_in, lb_logits, hg_gain, layer, batch, seq):
    t = proj.shape[0]
    n_rt = seq // HG_ROWS

    def grp(group):
        return pl.BlockSpec((HG_ROWS, D_MODEL), lambda b, r: (b * n_rt + r, group))

    return pl.pallas_call(
        functools.partial(_hgrn_kernel, layer=layer),
        grid=(batch, n_rt),
        in_specs=[
            grp(P_HQ), grp(0), grp(P_HI), grp(P_HG),
            pl.BlockSpec((D_MODEL, D_MODEL), lambda b, r: (0, G_HF), pipeline_mode=pl.Buffered(1)),
            pl.BlockSpec(lb_logits.shape, lambda b, r: (0, 0)),
            pl.BlockSpec((1, D_MODEL), lambda b, r: (0, 0)),
        ],
        out_specs=grp(0),
        out_shape=jax.ShapeDtypeStruct((t, HG_HEADS * HG_DV), bf16),
        scratch_shapes=[
            pltpu.VMEM((HG_HEADS, HG_DV, HG_DK), f32),
            pltpu.VMEM((D_MODEL, D_MODEL), bf16),
            pltpu.VMEM((2, 3, HG_PAR * HG_CHUNK, D_MODEL), bf16),
        ],
        compiler_params=pltpu.CompilerParams(
            dimension_semantics=("arbitrary", "arbitrary"), vmem_limit_bytes=VMEM_LIMIT),
        name="hgrn2",
    )(proj, h, proj, proj, w_in, lb_logits, hg_gain.reshape(1, HG_HEADS * HG_DV))


def _pair_rms(blocks, gain, group_mean):
    ms = [_dot((x * x).astype(bf16), group_mean) for x in blocks]
    return [x * lax.rsqrt(m + EPS) * gain for x, m in zip(blocks, ms)]


def _softplus2(z):
    return jnp.maximum(z, 0.0) + jnp.log2(1.0 + jnp.exp2(-jnp.abs(z)))


def _sb_kernel(q_ref, k_ref, v_ref, qg_ref, kg_ref, o_ref, kn_scr, q2_scr, acc_scr, car_scr):
    step = pl.program_id(2)
    seq = k_ref.shape[0]
    n_sub = SB_QROWS // SB_T
    lane = lax.broadcasted_iota(jnp.int32, (1, LANES), 1)
    lo_mask = lane < SB_DH

    gr = lax.broadcasted_iota(jnp.int32, (LANES, LANES), 0) // SB_DH
    gc = lax.broadcasted_iota(jnp.int32, (LANES, LANES), 1) // SB_DH
    group_mean = jnp.where(gr == gc, 1.0 / SB_DH, 0.0).astype(bf16)

    @pl.when(step == 0)
    def _():
        prep = 512

        blocks = [k_ref[r * prep:(r + 1) * prep, :].astype(f32) for r in range(seq // prep)]
        for r, kn in enumerate(_pair_rms(blocks, kg_ref[...], group_mean)):
            kn_scr[r * prep:(r + 1) * prep, :] = kn.astype(bf16)

    q_scale = math.log2(math.e) / math.sqrt(SB_DH)
    qn = _pair_rms([q_ref[c * SB_T:(c + 1) * SB_T, :].astype(f32) for c in range(n_sub)],
                   qg_ref[...] * q_scale, group_mean)
    q2 = [jnp.concatenate([jnp.where(lo_mask, q, 0.0).astype(bf16),
                           jnp.where(lo_mask, 0.0, q).astype(bf16)], axis=0) for q in qn]

    row = lax.broadcasted_iota(jnp.int32, (2 * SB_T, SB_T), 0) % SB_T
    col = lax.broadcasted_iota(jnp.int32, (2 * SB_T, SB_T), 1)
    earlier = col < row
    urow = lax.broadcasted_iota(jnp.int32, (2 * SB_T, 2 * SB_T), 0)
    ucol = lax.broadcasted_iota(jnp.int32, (2 * SB_T, 2 * SB_T), 1)
    suffix2 = jnp.where((urow >= ucol) | (ucol >= SB_T), 1.0, 0.0).astype(bf16)[:SB_T]
    suffix_pair = jnp.where((urow >= ucol) & ((urow < SB_T) == (ucol < SB_T)) | (urow >= SB_T) & (ucol < SB_T),
                            1.0, 0.0).astype(bf16)

    first = step == 0
    k_start = [jnp.where(first, 0, step * n_sub - 1) * SB_T] + [
        (step * n_sub + c - 1) * SB_T for c in range(1, n_sub)]
    pair_keys = [pl.ds(pl.multiple_of(k0, SB_T), 2 * SB_T) for k0 in k_start]

    zs = [_dot_nt(q2[c], kn_scr[pair_keys[c], :]) for c in range(n_sub)]
    off = jnp.where(first, 0, SB_T)
    zl = [zs[c][:, :SB_T] for c in range(n_sub)]
    zd = [jnp.where(col + SB_T < row + off if c == 0 else earlier, zs[c][:, SB_T:], NEG_INF)
          for c in range(n_sub)]
    zl[0] = jnp.where(col < row + off, zl[0], NEG_INF)
    zs = [jnp.concatenate([zl[c], zd[c]], axis=1) for c in range(n_sub)]
    rs = [_dot(_softplus2(zs[c]).astype(bf16), suffix_pair) for c in range(n_sub)]

    def still_live(car, next_tile):
        m = jnp.min(jnp.min(car, axis=1, keepdims=True), axis=0, keepdims=True)
        return ((next_tile >= 0) & (m[0, 0] <= SB_DONE_LOG2)).astype(jnp.int32)

    accs, cars, lives = [], [], []
    for c in range(n_sub):
        a = jnp.exp2(zs[c] - rs[c])
        accs.append(_dot(a.astype(bf16), v_ref[pair_keys[c], :]))
        total = rs[c][:, 0:1]
        cars.append(jnp.broadcast_to(total, (2 * SB_T, SB_T)))
        lives.append(still_live(total, step * n_sub + c - 2))
    for c in range(n_sub):
        q2_scr[c] = q2[c]
        acc_scr[c] = accs[c]
        car_scr[c] = cars[c]

    def one_more_tile(c, d):
        kb = step * n_sub + c - d
        keys = pl.ds(pl.multiple_of(kb * SB_T, SB_T), SB_T)
        z = _dot_nt(q2_scr[c], kn_scr[keys, :])
        full = _dot(_softplus2(z).astype(bf16), suffix2)
        car = car_scr[c]
        a = jnp.exp2(z - (full[:, :SB_T] + car))
        acc_scr[c] += _dot(a.astype(bf16), v_ref[keys, :])
        car = car + full[:, SB_T:]
        car_scr[c] = car
        return still_live(car, kb - 1)

    def more(state):
        live = state[1]
        for flag in state[2:]:
            live = live | flag
        return live > 0

    def walk(state):
        d = state[0]
        flags = [lax.cond(state[1 + c] > 0, functools.partial(one_more_tile, c, d), lambda: jnp.int32(0))
                 for c in range(n_sub)]
        return (d + 1, *flags)

    lax.while_loop(more, walk, (jnp.int32(2), *lives))

    for c in range(n_sub):
        o_ref[c * SB_T:(c + 1) * SB_T, :] = jnp.where(
            lo_mask, acc_scr[c, :SB_T, :], acc_scr[c, SB_T:, :]).astype(o_ref.dtype)


def _sb_attn(proj, q_gain, k_gain, batch, seq):
    t = proj.shape[0]
    hpg = D_MODEL // LANES
    n_pairs = SB_HEADS // 2
    n_qt = seq // SB_QROWS
    n_sub = SB_QROWS // SB_T
    return pl.pallas_call(
        _sb_kernel,
        grid=(batch, n_pairs, n_qt),
        in_specs=[
            pl.BlockSpec((SB_QROWS, LANES), lambda b, p, i: (b * n_qt + i, P_SQ * hpg + p)),
            pl.BlockSpec((seq, LANES), lambda b, p, i: (b, P_SK * hpg + p)),
            pl.BlockSpec((seq, LANES), lambda b, p, i: (b, P_SV * hpg + p)),
            pl.BlockSpec((None, 1, LANES), lambda b, p, i: (p, 0, 0)),
            pl.BlockSpec((None, 1, LANES), lambda b, p, i: (p, 0, 0)),
        ],
        out_specs=pl.BlockSpec((SB_QROWS, LANES), lambda b, p, i: (b * n_qt + i, p)),
        out_shape=jax.ShapeDtypeStruct((t, SB_HEADS * SB_DH), bf16),
        scratch_shapes=[
            pltpu.VMEM((seq, LANES), bf16),
            pltpu.VMEM((n_sub, 2 * SB_T, LANES), bf16),
            pltpu.VMEM((n_sub, 2 * SB_T, LANES), f32),
            pltpu.VMEM((n_sub, 2 * SB_T, LANES), f32),
        ],
        compiler_params=pltpu.CompilerParams(
            dimension_semantics=("arbitrary", "arbitrary", "arbitrary"), vmem_limit_bytes=VMEM_LIMIT),
        name="sb_attn",
    )(proj, proj, proj, q_gain.reshape(n_pairs, 1, LANES), k_gain.reshape(n_pairs, 1, LANES))


def _merge_kernel(x_ref, ohg_ref, osb_ref, ga_ref, gb_ref, whg_ref, wsb_ref, wo_ref, o_ref):
    half = x_ref.shape[0] // 2
    parts = [slice(0, half), slice(half, 2 * half)]
    y_hg = [_dot(ohg_ref[r, :], whg_ref[...]) for r in parts]
    y_sb = [_dot(osb_ref[r, :], wsb_ref[...]) for r in parts]
    for r, yh, ys in zip(parts, y_hg, y_sb):
        mixed = (_sigmoid_tanh(ga_ref[r, :].astype(f32)) * yh
                 + _sigmoid_tanh(gb_ref[r, :].astype(f32)) * ys)
        o_ref[r, :] = x_ref[r, :] + _dot(mixed.astype(bf16), wo_ref[...])


def _merge(x2, o_hg, o_sb, proj, w_hg, w_sb, w_o, tm=1024):
    t = x2.shape[0]
    rows = lambda i: (i, 0)
    resident = dict(pipeline_mode=pl.Buffered(1))
    return pl.pallas_call(
        _merge_kernel,
        grid=(t // tm,),
        in_specs=[
            pl.BlockSpec((tm, D_MODEL), rows),
            pl.BlockSpec((tm, D_MODEL), rows),
            pl.BlockSpec((tm, D_MODEL), rows),
            pl.BlockSpec((tm, D_MODEL), lambda i: (i, P_GA)),
            pl.BlockSpec((tm, D_MODEL), lambda i: (i, P_GB)),
            pl.BlockSpec((D_MODEL, D_MODEL), lambda i: (0, 0), **resident),
            pl.BlockSpec((D_MODEL, D_MODEL), lambda i: (0, 0), **resident),
            pl.BlockSpec((D_MODEL, D_MODEL), lambda i: (0, 0), **resident),
        ],
        out_specs=pl.BlockSpec((tm, D_MODEL), rows),
        out_shape=jax.ShapeDtypeStruct((t, D_MODEL), f32),
        compiler_params=pltpu.CompilerParams(
            dimension_semantics=("arbitrary",), vmem_limit_bytes=VMEM_LIMIT),
        name="merge",
    )(x2, o_hg, o_sb, proj, proj, w_hg, w_sb, w_o)


def _ffn_kernel(x_ref, gain_ref, wi_ref, wd_ref, o_ref, act_scr):
    x = x_ref[...]
    ms = jnp.mean(x * x, axis=-1, keepdims=True)
    h = (x * lax.rsqrt(ms + EPS) * gain_ref[...]).astype(bf16)
    for j in range(FF_HIDDEN // FF_CHUNK):
        cols = slice(j * FF_CHUNK, (j + 1) * FF_CHUNK)
        up_cols = slice(FF_HIDDEN + j * FF_CHUNK, FF_HIDDEN + (j + 1) * FF_CHUNK)
        gate = _dot(h, wi_ref[:, cols])
        up = _dot(h, wi_ref[:, up_cols])
        act_scr[:, cols] = (gate * _sigmoid(gate) * up).astype(bf16)
    o_ref[...] = x + _dot(act_scr[...], wd_ref[...])


def _ffn(x1, gain, w_in_bf, w_out_bf, tm=1024):
    t = x1.shape[0]
    resident = dict(pipeline_mode=pl.Buffered(1))
    return pl.pallas_call(
        _ffn_kernel,
        grid=(t // tm,),
        in_specs=[
            pl.BlockSpec((tm, D_MODEL), lambda i: (i, 0)),
            pl.BlockSpec((1, D_MODEL), lambda i: (0, 0)),
            pl.BlockSpec((D_MODEL, 2 * FF_HIDDEN), lambda i: (0, 0), **resident),
            pl.BlockSpec((FF_HIDDEN, D_MODEL), lambda i: (0, 0), **resident),
        ],
        out_specs=pl.BlockSpec((tm, D_MODEL), lambda i: (i, 0)),
        out_shape=jax.ShapeDtypeStruct((t, D_MODEL), f32),
        scratch_shapes=[pltpu.VMEM((tm, FF_HIDDEN), bf16)],
        compiler_params=pltpu.CompilerParams(
            dimension_semantics=("arbitrary",), vmem_limit_bytes=VMEM_LIMIT),
        name="ffn",
    )(x1, gain, w_in_bf, w_out_bf)


def kernel(x, norm1_gain, w_in, lb_logits, hg_out_norm, sb_q_norm, sb_k_norm,
           w_hg_out, w_sb_out, w_o, norm2_gain, w_ffn_in, w_ffn_out):
    batch, seq, d = x.shape
    depth = norm1_gain.shape[0]
    x2 = x.reshape(batch * seq, d)
    for layer in range(depth):
        gain1 = norm1_gain[layer][None, :]
        proj, h = _in_proj(x2, gain1, w_in[layer])
        o_hg = _hgrn2(proj, h, w_in[layer], lb_logits, hg_out_norm[layer], layer, batch, seq)
        o_sb = _sb_attn(proj, sb_q_norm[layer], sb_k_norm[layer], batch, seq)
        x1 = _merge(x2, o_hg, o_sb, proj, w_hg_out[layer].astype(bf16),
                    w_sb_out[layer].astype(bf16), w_o[layer].astype(bf16))
        x2 = _ffn(x1, norm2_gain[layer][None, :], w_ffn_in[layer].astype(bf16),
                  w_ffn_out[layer].astype(bf16))
    return x2.reshape(batch, seq, d)
```

```python
import functools
import math

import jax
import jax.numpy as jnp
from jax import lax
from jax.experimental import pallas as pl
from jax.experimental.pallas import tpu as pltpu

D_MODEL = 1024
HG_HEADS = 8
HG_DK = 128
HG_DV = 128
SB_HEADS = 16
SB_DH = 64
FF_HIDDEN = 2816
FF_CHUNK = 256
EPS = 1e-6
G_HQ, G_HF, G_HI, G_HG, G_SQ, G_SK, G_SV, G_GA, G_GB = range(9)
P_SOURCE = (G_HQ, G_HI, G_HG, G_SQ, G_SK, G_SV, G_GA, G_GB)
P_HQ, P_HI, P_HG, P_SQ, P_SK, P_SV, P_GA, P_GB = range(8)

LANES = 128
HG_CHUNK = 128
HG_PAR = 4
HG_ROWS = 1024
HG_EXP_CLAMP = 80.0
SB_T = 128
SB_QROWS = 2048
SB_DONE_LOG2 = 128.0
VMEM_LIMIT = 56 * 1024 * 1024

f32 = jnp.float32
bf16 = jnp.bfloat16
NEG_INF = float("-inf")


def _sigmoid(x):
    return 1.0 / (1.0 + jnp.exp(-x))


def _sigmoid_tanh(x):
    return 0.5 * jnp.tanh(0.5 * x) + 0.5


def _dot(a, b):
    return jnp.dot(a, b, preferred_element_type=f32)


def _dot_nt(a, b):
    return lax.dot_general(a, b, (((1,), (1,)), ((), ())), preferred_element_type=f32)


def _dot_tn(a, b):
    return lax.dot_general(a, b, (((0,), (0,)), ((), ())), preferred_element_type=f32)


def _split2(x):
    hi = x.astype(bf16)
    lo = (x - hi.astype(f32)).astype(bf16)
    return hi, lo


def _rms_norm_bf16(x, gain):
    ms = jnp.mean(x * x, axis=-1, keepdims=True)
    return (x * lax.rsqrt(ms + EPS) * gain).astype(bf16)


def _inproj_kernel(x_ref, gain_ref, w_ref, o_ref, h_ref):
    j = pl.program_id(1)

    @pl.when(j == 0)
    def _():
        w = w_ref[...].astype(bf16)
        n_blk = 4
        blk = x_ref.shape[0] // n_blk
        for b in range(n_blk):
            rows = slice(b * blk, (b + 1) * blk)
            h = _rms_norm_bf16(x_ref[rows, :], gain_ref[...])
            h_ref[rows, :] = h
            o_ref[rows, :] = _dot(h, w).astype(o_ref.dtype)

    is_gate = j == P_HG

    @pl.when(is_gate)
    def _():
        o_ref[...] = _sigmoid_tanh(_dot(h_ref[...], w_ref[...].astype(bf16))).astype(o_ref.dtype)

    @pl.when(jnp.logical_not(is_gate) & (j > 0))
    def _():
        o_ref[...] = _dot(h_ref[...], w_ref[...].astype(bf16)).astype(o_ref.dtype)


def _in_proj(x2, gain, w_in, tm=2048):
    t = x2.shape[0]
    n_groups = len(P_SOURCE)
    return pl.pallas_call(
        _inproj_kernel,
        grid=(t // tm, n_groups),
        in_specs=[
            pl.BlockSpec((tm, D_MODEL), lambda i, j: (i, 0)),
            pl.BlockSpec((1, D_MODEL), lambda i, j: (0, 0)),
            pl.BlockSpec((D_MODEL, D_MODEL), lambda i, j: (0, jnp.where(j >= G_HF, j + 1, j))),
        ],
        out_specs=[
            pl.BlockSpec((tm, D_MODEL), lambda i, j: (i, j)),
            pl.BlockSpec((tm, D_MODEL), lambda i, j: (i, 0)),
        ],
        out_shape=[
            jax.ShapeDtypeStruct((t, n_groups * D_MODEL), bf16),
            jax.ShapeDtypeStruct((t, D_MODEL), bf16),
        ],
        compiler_params=pltpu.CompilerParams(
            dimension_semantics=("arbitrary", "arbitrary"), vmem_limit_bytes=VMEM_LIMIT),
        name="in_proj",
    )(x2, gain, w_in)


def _hgrn_kernel(q_ref, h_ref, i_ref, og_ref, w_ref, lbl_ref, gain_ref, o_ref, st_scr, w_scr, fg_scr,
                 *, layer):
    c_len = HG_CHUNK
    blk = HG_PAR * c_len
    n_blk = q_ref.shape[0] // blk

    @pl.when((pl.program_id(0) == 0) & (pl.program_id(1) == 0))
    def _():
        w_scr[...] = w_ref[...].astype(bf16)

    @pl.when(pl.program_id(1) == 0)
    def _():
        st_scr[...] = jnp.zeros_like(st_scr)

    l = lbl_ref[...]
    e = jnp.exp(l - jnp.max(l, axis=0, keepdims=True))
    lb = jnp.sum(e[0:layer + 1, :], axis=0, keepdims=True) / jnp.sum(e, axis=0, keepdims=True)
    f_mid = 0.5 * (1.0 + lb)
    f_amp = 0.5 * (1.0 - lb)
    gain = gain_ref[...]

    def forget_block(it, slot):
        rows = pl.ds(pl.multiple_of(it * blk, blk), blk)
        f = f_mid + f_amp * jnp.tanh(0.5 * _dot(h_ref[rows, :], w_scr[...]))
        fg_scr[slot, 0], fg_scr[slot, 1] = _split2(jnp.log(f))
        fg_scr[slot, 2] = (1.0 - f).astype(bf16)

    row = lax.broadcasted_iota(jnp.int32, (c_len, c_len), 0)
    col = lax.broadcasted_iota(jnp.int32, (c_len, c_len), 1)
    causal = row >= col
    tri = jnp.where(causal, 1.0, 0.0).astype(bf16)
    tri2 = jnp.concatenate([tri, tri], axis=1)
    heads = range(HG_HEADS)
    sl = [slice(h * HG_DK, (h + 1) * HG_DK) for h in heads]

    def chunk_operands(it, p, slot):
        rows = pl.ds(pl.multiple_of(it * blk + p * c_len, c_len), c_len)
        in_blk = slice(p * c_len, (p + 1) * c_len)
        cum = _dot(tri2, jnp.concatenate([fg_scr[slot, 0, in_blk, :], fg_scr[slot, 1, in_blk, :]], axis=0))
        k = fg_scr[slot, 2, in_blk, :].astype(f32)
        last = cum[c_len - 1:c_len, :]
        mid = 0.5 * last
        q = q_ref[rows, :].astype(f32)
        return dict(
            rows=rows,
            qt=(q * jnp.exp(jnp.minimum(cum - mid, HG_EXP_CLAMP))).astype(bf16),
            kt=(k * jnp.exp(jnp.minimum(mid - cum, HG_EXP_CLAMP))).astype(bf16),
            qd=(q * jnp.exp(cum)).astype(bf16),
            kd=(k * jnp.exp(last - cum)).astype(bf16),
            e_last=jnp.exp(last),
            v=i_ref[rows, :],
            out_scale=gain * og_ref[rows, :].astype(f32),
        )

    def row_block(it, has_next):
        slot = it % 2
        cs = [chunk_operands(it, p, slot) for p in range(HG_PAR)]
        if has_next:
            forget_block(it + 1, 1 - slot)
        scores =[[jnp.where(causal, _dot_nt(c["qt"][:, sl[h]], c["kt"][:, sl[h]]), 0.0).astype(bf16)
                   for h in heads] for c in cs]
        upd = [[_dot_tn(c["v"][:, sl[h]], c["kd"][:, sl[h]]) for h in heads] for c in cs]
        states = [st_scr[h] for h in heads]
        inter = []
        for p, c in enumerate(cs):
            inter.append([_dot_nt(c["qd"][:, sl[h]], states[h].astype(bf16)) for h in heads])
            states = [states[h] * c["e_last"][:, sl[h]] + upd[p][h] for h in heads]
        for h in heads:
            st_scr[h] = states[h]
        for p, c in enumerate(cs):
            outs = []
            for h in heads:
                o = _dot(scores[p][h], c["v"][:, sl[h]]) + inter[p][h]
                ms = jnp.mean(o * o, axis=-1, keepdims=True)
                outs.append((o * lax.rsqrt(ms + EPS) * c["out_scale"][:, sl[h]]).astype(o_ref.dtype))
            o_ref[c["rows"], :] = jnp.concatenate(outs, axis=1)

    def body(it, carry):
        row_block(it, True)
        return carry

    forget_block(0, 0)
    lax.fori_loop(0, n_blk - 1, body, 0)
    row_block(n_blk - 1, False)


def _hgrn2(proj, h, w_in, lb_logits, hg_gain, layer, batch, seq):
    t = proj.shape[0]
    n_rt = seq // HG_ROWS

    def grp(group):
        return pl.BlockSpec((HG_ROWS, D_MODEL), lambda b, r: (b * n_rt + r, group))

    return pl.pallas_call(
        functools.partial(_hgrn_kernel, layer=layer),
        grid=(batch, n_rt),
        in_specs=[
            grp(P_HQ), grp(0), grp(P_HI), grp(P_HG),
            pl.BlockSpec((D_MODEL, D_MODEL), lambda b, r: (0, G_HF), pipeline_mode=pl.Buffered(1)),
            pl.BlockSpec(lb_logits.shape, lambda b, r: (0, 0)),
            pl.BlockSpec((1, D_MODEL), lambda b, r: (0, 0)),
        ],
        out_specs=grp(0),
        out_shape=jax.ShapeDtypeStruct((t, HG_HEADS * HG_DV), bf16),
        scratch_shapes=[
            pltpu.VMEM((HG_HEADS, HG_DV, HG_DK), f32),
            pltpu.VMEM((D_MODEL, D_MODEL), bf16),
            pltpu.VMEM((2, 3, HG_PAR * HG_CHUNK, D_MODEL), bf16),
        ],
        compiler_params=pltpu.CompilerParams(
            dimension_semantics=("arbitrary", "arbitrary"), vmem_limit_bytes=VMEM_LIMIT),
        name="hgrn2",
    )(proj, h, proj, proj, w_in, lb_logits, hg_gain.reshape(1, HG_HEADS * HG_DV))


def _pair_rms(blocks, gain, group_mean):
    ms = [_dot((x * x).astype(bf16), group_mean) for x in blocks]
    return [x * lax.rsqrt(m + EPS) * gain for x, m in zip(blocks, ms)]


def _softplus2(z):
    return jnp.maximum(z, 0.0) + jnp.log2(1.0 + jnp.exp2(-jnp.abs(z)))


def _sb_kernel(q_ref, k_ref, v_ref, qg_ref, kg_ref, o_ref, kn_scr, q2_scr, acc_scr, car_scr):
    step = pl.program_id(2)
    seq = k_ref.shape[0]
    n_sub = SB_QROWS // SB_T
    lane = lax.broadcasted_iota(jnp.int32, (1, LANES), 1)
    lo_mask = lane < SB_DH

    gr = lax.broadcasted_iota(jnp.int32, (LANES, LANES), 0) // SB_DH
    gc = lax.broadcasted_iota(jnp.int32, (LANES, LANES), 1) // SB_DH
    group_mean = jnp.where(gr == gc, 1.0 / SB_DH, 0.0).astype(bf16)

    @pl.when(step == 0)
    def _():
        prep = 512

        blocks = [k_ref[r * prep:(r + 1) * prep, :].astype(f32) for r in range(seq // prep)]
        for r, kn in enumerate(_pair_rms(blocks, kg_ref[...], group_mean)):
            kn_scr[r * prep:(r + 1) * prep, :] = kn.astype(bf16)

    q_scale = math.log2(math.e) / math.sqrt(SB_DH)
    qn = _pair_rms([q_ref[c * SB_T:(c + 1) * SB_T, :].astype(f32) for c in range(n_sub)],
                   qg_ref[...] * q_scale, group_mean)
    q2 = [jnp.concatenate([jnp.where(lo_mask, q, 0.0).astype(bf16),
                           jnp.where(lo_mask, 0.0, q).astype(bf16)], axis=0) for q in qn]

    row = lax.broadcasted_iota(jnp.int32, (2 * SB_T, SB_T), 0) % SB_T
    col = lax.broadcasted_iota(jnp.int32, (2 * SB_T, SB_T), 1)
    earlier = col < row
    urow = lax.broadcasted_iota(jnp.int32, (2 * SB_T, 2 * SB_T), 0)
    ucol = lax.broadcasted_iota(jnp.int32, (2 * SB_T, 2 * SB_T), 1)
    suffix2 = jnp.where((urow >= ucol) | (ucol >= SB_T), 1.0, 0.0).astype(bf16)[:SB_T]
    suffix_pair = jnp.where((urow >= ucol) & ((urow < SB_T) == (ucol < SB_T)) | (urow >= SB_T) & (ucol < SB_T),
                            1.0, 0.0).astype(bf16)

    first = step == 0
    k_start = [jnp.where(first, 0, step * n_sub - 1) * SB_T] + [
        (step * n_sub + c - 1) * SB_T for c in range(1, n_sub)]
    pair_keys = [pl.ds(pl.multiple_of(k0, SB_T), 2 * SB_T) for k0 in k_start]

    zs = [_dot_nt(q2[c], kn_scr[pair_keys[c], :]) for c in range(n_sub)]
    off = jnp.where(first, 0, SB_T)
    zl = [zs[c][:, :SB_T] for c in range(n_sub)]
    zd = [jnp.where(col + SB_T < row + off if c == 0 else earlier, zs[c][:, SB_T:], NEG_INF)
          for c in range(n_sub)]
    zl[0] = jnp.where(col < row + off, zl[0], NEG_INF)
    zs = [jnp.concatenate([zl[c], zd[c]], axis=1) for c in range(n_sub)]
    rs = [_dot(_softplus2(zs[c]).astype(bf16), suffix_pair) for c in range(n_sub)]

    def still_live(car, next_tile):
        m = jnp.min(jnp.min(car, axis=1, keepdims=True), axis=0, keepdims=True)
        return ((next_tile >= 0) & (m[0, 0] <= SB_DONE_LOG2)).astype(jnp.int32)

    accs, cars, lives = [], [], []
    for c in range(n_sub):
        a = jnp.exp2(zs[c] - rs[c])
        accs.append(_dot(a.astype(bf16), v_ref[pair_keys[c], :]))
        total = rs[c][:, 0:1]
        cars.append(jnp.broadcast_to(total, (2 * SB_T, SB_T)))
        lives.append(still_live(total, step * n_sub + c - 2))
    for c in range(n_sub):
        q2_scr[c] = q2[c]
        acc_scr[c] = accs[c]
        car_scr[c] = cars[c]

    def one_more_tile(c, d):
        kb = step * n_sub + c - d
        keys = pl.ds(pl.multiple_of(kb * SB_T, SB_T), SB_T)
        z = _dot_nt(q2_scr[c], kn_scr[keys, :])
        full = _dot(_softplus2(z).astype(bf16), suffix2)
        car = car_scr[c]
        a = jnp.exp2(z - (full[:, :SB_T] + car))
        acc_scr[c] += _dot(a.astype(bf16), v_ref[keys, :])
        car = car + full[:, SB_T:]
        car_scr[c] = car
        return still_live(car, kb - 1)

    def more(state):
        live = state[1]
        for flag in state[2:]:
            live = live | flag
        return live > 0

    def walk(state):
        d = state[0]
        flags = [lax.cond(state[1 + c] > 0, functools.partial(one_more_tile, c, d), lambda: jnp.int32(0))
                 for c in range(n_sub)]
        return (d + 1, *flags)

    lax.while_loop(more, walk, (jnp.int32(2), *lives))

    for c in range(n_sub):
        o_ref[c * SB_T:(c + 1) * SB_T, :] = jnp.where(
            lo_mask, acc_scr[c, :SB_T, :], acc_scr[c, SB_T:, :]).astype(o_ref.dtype)


def _sb_attn(proj, q_gain, k_gain, batch, seq):
    t = proj.shape[0]
    hpg = D_MODEL // LANES
    n_pairs = SB_HEADS // 2
    n_qt = seq // SB_QROWS
    n_sub = SB_QROWS // SB_T
    return pl.pallas_call(
        _sb_kernel,
        grid=(batch, n_pairs, n_qt),
        in_specs=[
            pl.BlockSpec((SB_QROWS, LANES), lambda b, p, i: (b * n_qt + i, P_SQ * hpg + p)),
            pl.BlockSpec((seq, LANES), lambda b, p, i: (b, P_SK * hpg + p)),
            pl.BlockSpec((seq, LANES), lambda b, p, i: (b, P_SV * hpg + p)),
            pl.BlockSpec((None, 1, LANES), lambda b, p, i: (p, 0, 0)),
            pl.BlockSpec((None, 1, LANES), lambda b, p, i: (p, 0, 0)),
        ],
        out_specs=pl.BlockSpec((SB_QROWS, LANES), lambda b, p, i: (b * n_qt + i, p)),
        out_shape=jax.ShapeDtypeStruct((t, SB_HEADS * SB_DH), bf16),
        scratch_shapes=[
            pltpu.VMEM((seq, LANES), bf16),
            pltpu.VMEM((n_sub, 2 * SB_T, LANES), bf16),
            pltpu.VMEM((n_sub, 2 * SB_T, LANES), f32),
            pltpu.VMEM((n_sub, 2 * SB_T, LANES), f32),
        ],
        compiler_params=pltpu.CompilerParams(
            dimension_semantics=("arbitrary", "arbitrary", "arbitrary"), vmem_limit_bytes=VMEM_LIMIT),
        name="sb_attn",
    )(proj, proj, proj, q_gain.reshape(n_pairs, 1, LANES), k_gain.reshape(n_pairs, 1, LANES))


def _merge_kernel(x_ref, ohg_ref, osb_ref, ga_ref, gb_ref, whg_ref, wsb_ref, wo_ref, o_ref):
    half = x_ref.shape[0] // 2
    parts = [slice(0, half), slice(half, 2 * half)]
    y_hg = [_dot(ohg_ref[r, :], whg_ref[...]) for r in parts]
    y_sb = [_dot(osb_ref[r, :], wsb_ref[...]) for r in parts]
    for r, yh, ys in zip(parts, y_hg, y_sb):
        mixed = (_sigmoid_tanh(ga_ref[r, :].astype(f32)) * yh
                 + _sigmoid_tanh(gb_ref[r, :].astype(f32)) * ys)
        o_ref[r, :] = x_ref[r, :] + _dot(mixed.astype(bf16), wo_ref[...])


def _merge(x2, o_hg, o_sb, proj, w_hg, w_sb, w_o, tm=1024):
    t = x2.shape[0]
    rows = lambda i: (i, 0)
    resident = dict(pipeline_mode=pl.Buffered(1))
    return pl.pallas_call(
        _merge_kernel,
        grid=(t // tm,),
        in_specs=[
            pl.BlockSpec((tm, D_MODEL), rows),
            pl.BlockSpec((tm, D_MODEL), rows),
            pl.BlockSpec((tm, D_MODEL), rows),
            pl.BlockSpec((tm, D_MODEL), lambda i: (i, P_GA)),
            pl.BlockSpec((tm, D_MODEL), lambda i: (i, P_GB)),
            pl.BlockSpec((D_MODEL, D_MODEL), lambda i: (0, 0), **resident),
            pl.BlockSpec((D_MODEL, D_MODEL), lambda i: (0, 0), **resident),
            pl.BlockSpec((D_MODEL, D_MODEL), lambda i: (0, 0), **resident),
        ],
        out_specs=pl.BlockSpec((tm, D_MODEL), rows),
        out_shape=jax.ShapeDtypeStruct((t, D_MODEL), f32),
        compiler_params=pltpu.CompilerParams(
            dimension_semantics=("arbitrary",), vmem_limit_bytes=VMEM_LIMIT),
        name="merge",
    )(x2, o_hg, o_sb, proj, proj, w_hg, w_sb, w_o)


def _ffn_kernel(x_ref, gain_ref, wi_ref, wd_ref, o_ref, act_scr):
    x = x_ref[...]
    ms = jnp.mean(x * x, axis=-1, keepdims=True)
    h = (x * lax.rsqrt(ms + EPS) * gain_ref[...]).astype(bf16)
    for j in range(FF_HIDDEN // FF_CHUNK):
        cols = slice(j * FF_CHUNK, (j + 1) * FF_CHUNK)
        up_cols = slice(FF_HIDDEN + j * FF_CHUNK, FF_HIDDEN + (j + 1) * FF_CHUNK)
        gate = _dot(h, wi_ref[:, cols])
        up = _dot(h, wi_ref[:, up_cols])
        act_scr[:, cols] = (gate * _sigmoid(gate) * up).astype(bf16)
    o_ref[...] = x + _dot(act_scr[...], wd_ref[...])


def _ffn(x1, gain, w_in_bf, w_out_bf, tm=1024):
    t = x1.shape[0]
    resident = dict(pipeline_mode=pl.Buffered(1))
    return pl.pallas_call(
        _ffn_kernel,
        grid=(t // tm,),
        in_specs=[
            pl.BlockSpec((tm, D_MODEL), lambda i: (i, 0)),
            pl.BlockSpec((1, D_MODEL), lambda i: (0, 0)),
            pl.BlockSpec((D_MODEL, 2 * FF_HIDDEN), lambda i: (0, 0), **resident),
            pl.BlockSpec((FF_HIDDEN, D_MODEL), lambda i: (0, 0), **resident),
        ],
        out_specs=pl.BlockSpec((tm, D_MODEL), lambda i: (i, 0)),
        out_shape=jax.ShapeDtypeStruct((t, D_MODEL), f32),
        scratch_shapes=[pltpu.VMEM((tm, FF_HIDDEN), bf16)],
        compiler_params=pltpu.CompilerParams(
            dimension_semantics=("arbitrary",), vmem_limit_bytes=VMEM_LIMIT),
        name="ffn",
    )(x1, gain, w_in_bf, w_out_bf)


def kernel(x, norm1_gain, w_in, lb_logits, hg_out_norm, sb_q_norm, sb_k_norm,
           w_hg_out, w_sb_out, w_o, norm2_gain, w_ffn_in, w_ffn_out):
    batch, seq, d = x.shape
    depth = norm1_gain.shape[0]
    x2 = x.reshape(batch * seq, d)
    for layer in range(depth):
        gain1 = norm1_gain[layer][None, :]
        proj, h = _in_proj(x2, gain1, w_in[layer])
        o_hg = _hgrn2(proj, h, w_in[layer], lb_logits, hg_out_norm[layer], layer, batch, seq)
        o_sb = _sb_attn(proj, sb_q_norm[layer], sb_k_norm[layer], batch, seq)
        x1 = _merge(x2, o_hg, o_sb, proj, w_hg_out[layer].astype(bf16),
                    w_sb_out[layer].astype(bf16), w_o[layer].astype(bf16))
        x2 = _ffn(x1, norm2_gain[layer][None, :], w_ffn_in[layer].astype(bf16),
                  w_ffn_out[layer].astype(bf16))
    return x2.reshape(batch, seq, d)
```

```python
import functools
import math

import jax
import jax.numpy as jnp
from jax import lax
from jax.experimental import pallas as pl
from jax.experimental.pallas import tpu as pltpu

D_MODEL = 1024
HG_HEADS = 8
HG_DK = 128
HG_DV = 128
SB_HEADS = 16
SB_DH = 64
FF_HIDDEN = 2816
FF_CHUNK = 256
EPS = 1e-6
G_HQ, G_HF, G_HI, G_HG, G_SQ, G_SK, G_SV, G_GA, G_GB = range(9)
P_SOURCE = (G_HQ, G_HI, G_HG, G_SQ, G_SK, G_SV, G_GA, G_GB)
P_HQ, P_HI, P_HG, P_SQ, P_SK, P_SV, P_GA, P_GB = range(8)

LANES = 128
HG_CHUNK = 128
HG_PAR = 4
HG_ROWS = 1024
HG_EXP_CLAMP = 80.0
SB_T = 128
SB_QROWS = 2048
SB_DONE_LOG2 = 128.0
VMEM_LIMIT = 56 * 1024 * 1024

f32 = jnp.float32
bf16 = jnp.bfloat16
NEG_INF = float("-inf")


def _sigmoid(x):
    return 1.0 / (1.0 + jnp.exp(-x))


def _sigmoid_tanh(x):
    return 0.5 * jnp.tanh(0.5 * x) + 0.5


def _dot(a, b):
    return jnp.dot(a, b, preferred_element_type=f32)


def _dot_nt(a, b):
    return lax.dot_general(a, b, (((1,), (1,)), ((), ())), preferred_element_type=f32)


def _dot_tn(a, b):
    return lax.dot_general(a, b, (((0,), (0,)), ((), ())), preferred_element_type=f32)


def _split2(x):
    hi = x.astype(bf16)
    lo = (x - hi.astype(f32)).astype(bf16)
    return hi, lo


def _rms_norm_bf16(x, gain):
    ms = jnp.mean(x * x, axis=-1, keepdims=True)
    return (x * lax.rsqrt(ms + EPS) * gain).astype(bf16)


def _inproj_kernel(x_ref, gain_ref, w_ref, o_ref, h_ref):
    j = pl.program_id(1)

    @pl.when(j == 0)
    def _():
        w = w_ref[...].astype(bf16)
        n_blk = 4
        blk = x_ref.shape[0] // n_blk
        for b in range(n_blk):
            rows = slice(b * blk, (b + 1) * blk)
            h = _rms_norm_bf16(x_ref[rows, :], gain_ref[...])
            h_ref[rows, :] = h
            o_ref[rows, :] = _dot(h, w).astype(o_ref.dtype)

    is_gate = j == P_HG

    @pl.when(is_gate)
    def _():
        o_ref[...] = _sigmoid_tanh(_dot(h_ref[...], w_ref[...].astype(bf16))).astype(o_ref.dtype)

    @pl.when(jnp.logical_not(is_gate) & (j > 0))
    def _():
        o_ref[...] = _dot(h_ref[...], w_ref[...].astype(bf16)).astype(o_ref.dtype)


def _in_proj(x2, gain, w_in, tm=2048):
    t = x2.shape[0]
    n_groups = len(P_SOURCE)
    return pl.pallas_call(
        _inproj_kernel,
        grid=(t // tm, n_groups),
        in_specs=[
            pl.BlockSpec((tm, D_MODEL), lambda i, j: (i, 0)),
            pl.BlockSpec((1, D_MODEL), lambda i, j: (0, 0)),
            pl.BlockSpec((D_MODEL, D_MODEL), lambda i, j: (0, jnp.where(j >= G_HF, j + 1, j))),
        ],
        out_specs=[
            pl.BlockSpec((tm, D_MODEL), lambda i, j: (i, j)),
            pl.BlockSpec((tm, D_MODEL), lambda i, j: (i, 0)),
        ],
        out_shape=[
            jax.ShapeDtypeStruct((t, n_groups * D_MODEL), bf16),
            jax.ShapeDtypeStruct((t, D_MODEL), bf16),
        ],
        compiler_params=pltpu.CompilerParams(
            dimension_semantics=("arbitrary", "arbitrary"), vmem_limit_bytes=VMEM_LIMIT),
        name="in_proj",
    )(x2, gain, w_in)


def _hgrn_kernel(q_ref, h_ref, i_ref, og_ref, w_ref, lbl_ref, gain_ref, o_ref, st_scr, w_scr, fg_scr,
                 *, layer):
    c_len = HG_CHUNK
    blk = HG_PAR * c_len
    n_blk = q_ref.shape[0] // blk

    @pl.when((pl.program_id(0) == 0) & (pl.program_id(1) == 0))
    def _():
        w_scr[...] = w_ref[...].astype(bf16)

    @pl.when(pl.program_id(1) == 0)
    def _():
        st_scr[...] = jnp.zeros_like(st_scr)

    l = lbl_ref[...]
    e = jnp.exp(l - jnp.max(l, axis=0, keepdims=True))
    lb = jnp.sum(e[0:layer + 1, :], axis=0, keepdims=True) / jnp.sum(e, axis=0, keepdims=True)
    f_mid = 0.5 * (1.0 + lb)
    f_amp = 0.5 * (1.0 - lb)
    gain = gain_ref[...]

    def forget_block(it, slot):
        rows = pl.ds(pl.multiple_of(it * blk, blk), blk)
        f = f_mid + f_amp * jnp.tanh(0.5 * _dot(h_ref[rows, :], w_scr[...]))
        fg_scr[slot, 0], fg_scr[slot, 1] = _split2(jnp.log(f))
        fg_scr[slot, 2] = (1.0 - f).astype(bf16)

    row = lax.broadcasted_iota(jnp.int32, (c_len, c_len), 0)
    col = lax.broadcasted_iota(jnp.int32, (c_len, c_len), 1)
    causal = row >= col
    tri = jnp.where(causal, 1.0, 0.0).astype(bf16)
    tri2 = jnp.concatenate([tri, tri], axis=1)
    heads = range(HG_HEADS)
    sl = [slice(h * HG_DK, (h + 1) * HG_DK) for h in heads]

    def chunk_operands(it, p, slot):
        rows = pl.ds(pl.multiple_of(it * blk + p * c_len, c_len), c_len)
        in_blk = slice(p * c_len, (p + 1) * c_len)
        cum = _dot(tri2, jnp.concatenate([fg_scr[slot, 0, in_blk, :], fg_scr[slot, 1, in_blk, :]], axis=0))
        k = fg_scr[slot, 2, in_blk, :].astype(f32)
        last = cum[c_len - 1:c_len, :]
        mid = 0.5 * last
        q = q_ref[rows, :].astype(f32)
        return dict(
            rows=rows,
            qt=(q * jnp.exp(jnp.minimum(cum - mid, HG_EXP_CLAMP))).astype(bf16),
            kt=(k * jnp.exp(jnp.minimum(mid - cum, HG_EXP_CLAMP))).astype(bf16),
            qd=(q * jnp.exp(cum)).astype(bf16),
            kd=(k * jnp.exp(last - cum)).astype(bf16),
            e_last=jnp.exp(last),
            v=i_ref[rows, :],
            out_scale=gain * og_ref[rows, :].astype(f32),
        )

    def row_block(it, has_next):
        slot = it % 2
        cs = [chunk_operands(it, p, slot) for p in range(HG_PAR)]
        if has_next:
            forget_block(it + 1, 1 - slot)
        scores =[[jnp.where(causal, _dot_nt(c["qt"][:, sl[h]], c["kt"][:, sl[h]]), 0.0).astype(bf16)
                   for h in heads] for c in cs]
        upd = [[_dot_tn(c["v"][:, sl[h]], c["kd"][:, sl[h]]) for h in heads] for c in cs]
        states = [st_scr[h] for h in heads]
        inter = []
        for p, c in enumerate(cs):
            inter.append([_dot_nt(c["qd"][:, sl[h]], states[h].astype(bf16)) for h in heads])
            states = [states[h] * c["e_last"][:, sl[h]] + upd[p][h] for h in heads]
        for h in heads:
            st_scr[h] = states[h]
        for p, c in enumerate(cs):
            outs = []
            for h in heads:
                o = _dot(scores[p][h], c["v"][:, sl[h]]) + inter[p][h]
                ms = jnp.mean(o * o, axis=-1, keepdims=True)
                outs.append((o * lax.rsqrt(ms + EPS) * c["out_scale"][:, sl[h]]).astype(o_ref.dtype))
            o_ref[c["rows"], :] = jnp.concatenate(outs, axis=1)

    def body(it, carry):
        row_block(it, True)
        return carry

    forget_block(0, 0)
    lax.fori_loop(0, n_blk - 1, body, 0)
    row_block(n_blk - 1, False)


def _hgrn2(proj, h, w_in, lb_logits, hg_gain, layer, batch, seq):
    t = proj.shape[0]
    n_rt = seq // HG_ROWS

    def grp(group):
        return pl.BlockSpec((HG_ROWS, D_MODEL), lambda b, r: (b * n_rt + r, group))

    return pl.pallas_call(
        functools.partial(_hgrn_kernel, layer=layer),
        grid=(batch, n_rt),
        in_specs=[
            grp(P_HQ), grp(0), grp(P_HI), grp(P_HG),
            pl.BlockSpec((D_MODEL, D_MODEL), lambda b, r: (0, G_HF), pipeline_mode=pl.Buffered(1)),
            pl.BlockSpec(lb_logits.shape, lambda b, r: (0, 0)),
            pl.BlockSpec((1, D_MODEL), lambda b, r: (0, 0)),
        ],
        out_specs=grp(0),
        out_shape=jax.ShapeDtypeStruct((t, HG_HEADS * HG_DV), bf16),
        scratch_shapes=[
            pltpu.VMEM((HG_HEADS, HG_DV, HG_DK), f32),
            pltpu.VMEM((D_MODEL, D_MODEL), bf16),
            pltpu.VMEM((2, 3, HG_PAR * HG_CHUNK, D_MODEL), bf16),
        ],
        compiler_params=pltpu.CompilerParams(
            dimension_semantics=("arbitrary", "arbitrary"), vmem_limit_bytes=VMEM_LIMIT),
        name="hgrn2",
    )(proj, h, proj, proj, w_in, lb_logits, hg_gain.reshape(1, HG_HEADS * HG_DV))


def _pair_rms(blocks, gain, group_mean):
    ms = [_dot((x * x).astype(bf16), group_mean) for x in blocks]
    return [x * lax.rsqrt(m + EPS) * gain for x, m in zip(blocks, ms)]


def _softplus2(z):
    return jnp.maximum(z, 0.0) + jnp.log2(1.0 + jnp.exp2(-jnp.abs(z)))


def _sb_kernel(q_ref, k_ref, v_ref, qg_ref, kg_ref, o_ref, kn_scr, q2_scr, acc_scr, car_scr):
    step = pl.program_id(2)
    seq = k_ref.shape[0]
    n_sub = SB_QROWS // SB_T
    lane = lax.broadcasted_iota(jnp.int32, (1, LANES), 1)
    lo_mask = lane < SB_DH

    gr = lax.broadcasted_iota(jnp.int32, (LANES, LANES), 0) // SB_DH
    gc = lax.broadcasted_iota(jnp.int32, (LANES, LANES), 1) // SB_DH
    group_mean = jnp.where(gr == gc, 1.0 / SB_DH, 0.0).astype(bf16)

    @pl.when(step == 0)
    def _():
        prep = 512

        blocks = [k_ref[r * prep:(r + 1) * prep, :].astype(f32) for r in range(seq // prep)]
        for r, kn in enumerate(_pair_rms(blocks, kg_ref[...], group_mean)):
            kn_scr[r * prep:(r + 1) * prep, :] = kn.astype(bf16)

    q_scale = math.log2(math.e) / math.sqrt(SB_DH)
    qn = _pair_rms([q_ref[c * SB_T:(c + 1) * SB_T, :].astype(f32) for c in range(n_sub)],
                   qg_ref[...] * q_scale, group_mean)
    q2 = [jnp.concatenate([jnp.where(lo_mask, q, 0.0).astype(bf16),
                           jnp.where(lo_mask, 0.0, q).astype(bf16)], axis=0) for q in qn]

    row = lax.broadcasted_iota(jnp.int32, (2 * SB_T, SB_T), 0) % SB_T
    col = lax.broadcasted_iota(jnp.int32, (2 * SB_T, SB_T), 1)
    earlier = col < row
    urow = lax.broadcasted_iota(jnp.int32, (2 * SB_T, 2 * SB_T), 0)
    ucol = lax.broadcasted_iota(jnp.int32, (2 * SB_T, 2 * SB_T), 1)
    suffix2 = jnp.where((urow >= ucol) | (ucol >= SB_T), 1.0, 0.0).astype(bf16)[:SB_T]
    suffix_pair = jnp.where((urow >= ucol) & ((urow < SB_T) == (ucol < SB_T)) | (urow >= SB_T) & (ucol < SB_T),
                            1.0, 0.0).astype(bf16)

    first = step == 0
    k_start = [jnp.where(first, 0, step * n_sub - 1) * SB_T] + [
        (step * n_sub + c - 1) * SB_T for c in range(1, n_sub)]
    pair_keys = [pl.ds(pl.multiple_of(k0, SB_T), 2 * SB_T) for k0 in k_start]

    zs = [_dot_nt(q2[c], kn_scr[pair_keys[c], :]) for c in range(n_sub)]
    off = jnp.where(first, 0, SB_T)
    zl = [zs[c][:, :SB_T] for c in range(n_sub)]
    zd = [jnp.where(col + SB_T < row + off if c == 0 else earlier, zs[c][:, SB_T:], NEG_INF)
          for c in range(n_sub)]
    zl[0] = jnp.where(col < row + off, zl[0], NEG_INF)
    zs = [jnp.concatenate([zl[c], zd[c]], axis=1) for c in range(n_sub)]
    rs = [_dot(_softplus2(zs[c]).astype(bf16), suffix_pair) for c in range(n_sub)]

    def still_live(car, next_tile):
        m = jnp.min(jnp.min(car, axis=1, keepdims=True), axis=0, keepdims=True)
        return ((next_tile >= 0) & (m[0, 0] <= SB_DONE_LOG2)).astype(jnp.int32)

    accs, cars, lives = [], [], []
    for c in range(n_sub):
        a = jnp.exp2(zs[c] - rs[c])
        accs.append(_dot(a.astype(bf16), v_ref[pair_keys[c], :]))
        total = rs[c][:, 0:1]
        cars.append(jnp.broadcast_to(total, (2 * SB_T, SB_T)))
        lives.append(still_live(total, step * n_sub + c - 2))
    for c in range(n_sub):
        q2_scr[c] = q2[c]
        acc_scr[c] = accs[c]
        car_scr[c] = cars[c]

    def one_more_tile(c, d):
        kb = step * n_sub + c - d
        keys = pl.ds(pl.multiple_of(kb * SB_T, SB_T), SB_T)
        z = _dot_nt(q2_scr[c], kn_scr[keys, :])
        full = _dot(_softplus2(z).astype(bf16), suffix2)
        car = car_scr[c]
        a = jnp.exp2(z - (full[:, :SB_T] + car))
        acc_scr[c] += _dot(a.astype(bf16), v_ref[keys, :])
        car = car + full[:, SB_T:]
        car_scr[c] = car
        return still_live(car, kb - 1)

    def more(state):
        live = state[1]
        for flag in state[2:]:
            live = live | flag
        return live > 0

    def walk(state):
        d = state[0]
        flags = [lax.cond(state[1 + c] > 0, functools.partial(one_more_tile, c, d), lambda: jnp.int32(0))
                 for c in range(n_sub)]
        return (d + 1, *flags)

    lax.while_loop(more, walk, (jnp.int32(2), *lives))

    for c in range(n_sub):
        o_ref[c * SB_T:(c + 1) * SB_T, :] = jnp.where(
            lo_mask, acc_scr[c, :SB_T, :], acc_scr[c, SB_T:, :]).astype(o_ref.dtype)


def _sb_attn(proj, q_gain, k_gain, batch, seq):
    t = proj.shape[0]
    hpg = D_MODEL // LANES
    n_pairs = SB_HEADS // 2
    n_qt = seq // SB_QROWS
    n_sub = SB_QROWS // SB_T
    return pl.pallas_call(
        _sb_kernel,
        grid=(batch, n_pairs, n_qt),
        in_specs=[
            pl.BlockSpec((SB_QROWS, LANES), lambda b, p, i: (b * n_qt + i, P_SQ * hpg + p)),
            pl.BlockSpec((seq, LANES), lambda b, p, i: (b, P_SK * hpg + p)),
            pl.BlockSpec((seq, LANES), lambda b, p, i: (b, P_SV * hpg + p)),
            pl.BlockSpec((None, 1, LANES), lambda b, p, i: (p, 0, 0)),
            pl.BlockSpec((None, 1, LANES), lambda b, p, i: (p, 0, 0)),
        ],
        out_specs=pl.BlockSpec((SB_QROWS, LANES), lambda b, p, i: (b * n_qt + i, p)),
        out_shape=jax.ShapeDtypeStruct((t, SB_HEADS * SB_DH), bf16),
        scratch_shapes=[
            pltpu.VMEM((seq, LANES), bf16),
            pltpu.VMEM((n_sub, 2 * SB_T, LANES), bf16),
            pltpu.VMEM((n_sub, 2 * SB_T, LANES), f32),
            pltpu.VMEM((n_sub, 2 * SB_T, LANES), f32),
        ],
        compiler_params=pltpu.CompilerParams(
            dimension_semantics=("arbitrary", "arbitrary", "arbitrary"), vmem_limit_bytes=VMEM_LIMIT),
        name="sb_attn",
    )(proj, proj, proj, q_gain.reshape(n_pairs, 1, LANES), k_gain.reshape(n_pairs, 1, LANES))


def _merge_kernel(x_ref, ohg_ref, osb_ref, ga_ref, gb_ref, whg_ref, wsb_ref, wo_ref, o_ref):
    half = x_ref.shape[0] // 2
    parts = [slice(0, half), slice(half, 2 * half)]
    y_hg = [_dot(ohg_ref[r, :], whg_ref[...]) for r in parts]
    y_sb = [_dot(osb_ref[r, :], wsb_ref[...]) for r in parts]
    for r, yh, ys in zip(parts, y_hg, y_sb):
        mixed = (_sigmoid_tanh(ga_ref[r, :].astype(f32)) * yh
                 + _sigmoid_tanh(gb_ref[r, :].astype(f32)) * ys)
        o_ref[r, :] = x_ref[r, :] + _dot(mixed.astype(bf16), wo_ref[...])


def _merge(x2, o_hg, o_sb, proj, w_hg, w_sb, w_o, tm=1024):
    t = x2.shape[0]
    rows = lambda i: (i, 0)
    resident = dict(pipeline_mode=pl.Buffered(1))
    return pl.pallas_call(
        _merge_kernel,
        grid=(t // tm,),
        in_specs=[
            pl.BlockSpec((tm, D_MODEL), rows),
            pl.BlockSpec((tm, D_MODEL), rows),
            pl.BlockSpec((tm, D_MODEL), rows),
            pl.BlockSpec((tm, D_MODEL), lambda i: (i, P_GA)),
            pl.BlockSpec((tm, D_MODEL), lambda i: (i, P_GB)),
            pl.BlockSpec((D_MODEL, D_MODEL), lambda i: (0, 0), **resident),
            pl.BlockSpec((D_MODEL, D_MODEL), lambda i: (0, 0), **resident),
            pl.BlockSpec((D_MODEL, D_MODEL), lambda i: (0, 0), **resident),
        ],
        out_specs=pl.BlockSpec((tm, D_MODEL), rows),
        out_shape=jax.ShapeDtypeStruct((t, D_MODEL), f32),
        compiler_params=pltpu.CompilerParams(
            dimension_semantics=("arbitrary",), vmem_limit_bytes=VMEM_LIMIT),
        name="merge",
    )(x2, o_hg, o_sb, proj, proj, w_hg, w_sb, w_o)


def _ffn_kernel(x_ref, gain_ref, wi_ref, wd_ref, o_ref, h_scr, act_scr):
    def swiglu(h, rows, j):
        cols = slice(j * FF_CHUNK, (j + 1) * FF_CHUNK)
        up_cols = slice(FF_HIDDEN + j * FF_CHUNK, FF_HIDDEN + (j + 1) * FF_CHUNK)
        gate = _dot(h, wi_ref[:, cols])
        up = _dot(h, wi_ref[:, up_cols])
        act_scr[rows, cols] = (gate * _sigmoid(gate) * up).astype(bf16)

    n_blk = 4
    blk = x_ref.shape[0] // n_blk
    for b in range(n_blk):
        rows = slice(b * blk, (b + 1) * blk)
        h = _rms_norm_bf16(x_ref[rows, :], gain_ref[...])
        h_scr[rows, :] = h
        swiglu(h, rows, 0)
    for j in range(1, FF_HIDDEN // FF_CHUNK):
        swiglu(h_scr[...], slice(None), j)
    o_ref[...] = x_ref[...] + _dot(act_scr[...], wd_ref[...])


def _ffn(x1, gain, w_in_bf, w_out_bf, tm=1024):
    t = x1.shape[0]
    resident = dict(pipeline_mode=pl.Buffered(1))
    return pl.pallas_call(
        _ffn_kernel,
        grid=(t // tm,),
        in_specs=[
            pl.BlockSpec((tm, D_MODEL), lambda i: (i, 0)),
            pl.BlockSpec((1, D_MODEL), lambda i: (0, 0)),
            pl.BlockSpec((D_MODEL, 2 * FF_HIDDEN), lambda i: (0, 0), **resident),
            pl.BlockSpec((FF_HIDDEN, D_MODEL), lambda i: (0, 0), **resident),
        ],
        out_specs=pl.BlockSpec((tm, D_MODEL), lambda i: (i, 0)),
        out_shape=jax.ShapeDtypeStruct((t, D_MODEL), f32),
        scratch_shapes=[pltpu.VMEM((tm, D_MODEL), bf16), pltpu.VMEM((tm, FF_HIDDEN), bf16)],
        compiler_params=pltpu.CompilerParams(
            dimension_semantics=("arbitrary",), vmem_limit_bytes=VMEM_LIMIT),
        name="ffn",
    )(x1, gain, w_in_bf, w_out_bf)


def kernel(x, norm1_gain, w_in, lb_logits, hg_out_norm, sb_q_norm, sb_k_norm,
           w_hg_out, w_sb_out, w_o, norm2_gain, w_ffn_in, w_ffn_out):
    batch, seq, d = x.shape
    depth = norm1_gain.shape[0]
    x2 = x.reshape(batch * seq, d)
    for layer in range(depth):
        gain1 = norm1_gain[layer][None, :]
        proj, h = _in_proj(x2, gain1, w_in[layer])
        o_hg = _hgrn2(proj, h, w_in[layer], lb_logits, hg_out_norm[layer], layer, batch, seq)
        o_sb = _sb_attn(proj, sb_q_norm[layer], sb_k_norm[layer], batch, seq)
        x1 = _merge(x2, o_hg, o_sb, proj, w_hg_out[layer].astype(bf16),
                    w_sb_out[layer].astype(bf16), w_o[layer].astype(bf16))
        x2 = _ffn(x1, norm2_gain[layer][None, :], w_ffn_in[layer].astype(bf16),
                  w_ffn_out[layer].astype(bf16))
    return x2.reshape(batch, seq, d)
```

```python
import functools
import math

import jax
import jax.numpy as jnp
from jax import lax
from jax.experimental import pallas as pl
from jax.experimental.pallas import tpu as pltpu

D_MODEL = 1024
HG_HEADS = 8
HG_DK = 128
HG_DV = 128
SB_HEADS = 16
SB_DH = 64
FF_HIDDEN = 2816
FF_CHUNK = 256
EPS = 1e-6
G_HQ, G_HF, G_HI, G_HG, G_SQ, G_SK, G_SV, G_GA, G_GB = range(9)
P_SOURCE = (G_HQ, G_HI, G_HG, G_SQ, G_SK, G_SV, G_GA, G_GB)
P_HQ, P_HI, P_HG, P_SQ, P_SK, P_SV, P_GA, P_GB = range(8)

LANES = 128
HG_CHUNK = 128
HG_PAR = 4
HG_ROWS = 1024
HG_EXP_CLAMP = 80.0
SB_T = 128
SB_QROWS = 2048
SB_DONE_LOG2 = 128.0
VMEM_LIMIT = 56 * 1024 * 1024

f32 = jnp.float32
bf16 = jnp.bfloat16
NEG_INF = float("-inf")


def _sigmoid(x):
    return 1.0 / (1.0 + jnp.exp(-x))


def _sigmoid_tanh(x):
    return 0.5 * jnp.tanh(0.5 * x) + 0.5


def _dot(a, b):
    return jnp.dot(a, b, preferred_element_type=f32)


def _dot_nt(a, b):
    return lax.dot_general(a, b, (((1,), (1,)), ((), ())), preferred_element_type=f32)


def _dot_tn(a, b):
    return lax.dot_general(a, b, (((0,), (0,)), ((), ())), preferred_element_type=f32)


def _split2(x):
    hi = x.astype(bf16)
    lo = (x - hi.astype(f32)).astype(bf16)
    return hi, lo


def _rms_norm_bf16(x, gain):
    ms = jnp.mean(x * x, axis=-1, keepdims=True)
    return (x * lax.rsqrt(ms + EPS) * gain).astype(bf16)


def _inproj_kernel(x_ref, gain_ref, w_hbm, o_hbm, h_ref, w_buf, o_buf, w_sem, o_sem):
    tm = x_ref.shape[0]
    n_groups = len(P_SOURCE)
    row0 = pl.multiple_of(pl.program_id(0) * tm, tm)

    def w_copy(g):
        cols = pl.ds(P_SOURCE[g] * D_MODEL, D_MODEL)
        return pltpu.make_async_copy(w_hbm.at[:, cols], w_buf.at[g % 2], w_sem.at[g % 2])

    def o_copy(g):
        dst = o_hbm.at[pl.ds(row0, tm), pl.ds(g * D_MODEL, D_MODEL)]
        return pltpu.make_async_copy(o_buf.at[g % 2], dst, o_sem.at[g % 2])

    w_copy(0).start()
    w_copy(1).start()
    w_copy(0).wait()
    w = w_buf[0].astype(bf16)
    for g in range(n_groups):
        if g >= 2:
            o_copy(g - 2).wait()
        if g + 1 < n_groups:
            w_copy(g + 1).wait()
            w_next = w_buf[(g + 1) % 2].astype(bf16)
        if g == 0:
            n_blk = 4
            blk = tm // n_blk
            for b in range(n_blk):
                rows = slice(b * blk, (b + 1) * blk)
                h = _rms_norm_bf16(x_ref[rows, :], gain_ref[...])
                h_ref[rows, :] = h
                o_buf[0, rows, :] = _dot(h, w).astype(bf16)
        elif g == P_HG:
            o_buf[g % 2] = _sigmoid_tanh(_dot(h_ref[...], w)).astype(bf16)
        else:
            o_buf[g % 2] = _dot(h_ref[...], w).astype(bf16)
        o_copy(g).start()
        if g + 2 < n_groups:
            w_copy(g + 2).start()
        if g + 1 < n_groups:
            w = w_next
    o_copy(n_groups - 2).wait()
    o_copy(n_groups - 1).wait()


def _in_proj(x2, gain, w_in, tm=2048):
    t = x2.shape[0]
    n_groups = len(P_SOURCE)
    return pl.pallas_call(
        _inproj_kernel,
        grid=(t // tm,),
        in_specs=[
            pl.BlockSpec((tm, D_MODEL), lambda i: (i, 0)),
            pl.BlockSpec((1, D_MODEL), lambda i: (0, 0)),
            pl.BlockSpec(memory_space=pl.ANY),
        ],
        out_specs=[
            pl.BlockSpec(memory_space=pl.ANY),
            pl.BlockSpec((tm, D_MODEL), lambda i: (i, 0)),
        ],
        out_shape=[
            jax.ShapeDtypeStruct((t, n_groups * D_MODEL), bf16),
            jax.ShapeDtypeStruct((t, D_MODEL), bf16),
        ],
        scratch_shapes=[
            pltpu.VMEM((2, D_MODEL, D_MODEL), f32),
            pltpu.VMEM((2, tm, D_MODEL), bf16),
            pltpu.SemaphoreType.DMA((2,)),
            pltpu.SemaphoreType.DMA((2,)),
        ],
        compiler_params=pltpu.CompilerParams(
            dimension_semantics=("arbitrary",), vmem_limit_bytes=VMEM_LIMIT),
        name="in_proj",
    )(x2, gain, w_in)


def _hgrn_kernel(q_ref, h_ref, i_ref, og_ref, w_ref, lbl_ref, gain_ref, o_ref, st_scr, w_scr, fg_scr,
                 *, layer):
    c_len = HG_CHUNK
    blk = HG_PAR * c_len
    n_blk = q_ref.shape[0] // blk

    @pl.when((pl.program_id(0) == 0) & (pl.program_id(1) == 0))
    def _():
        w_scr[...] = w_ref[...].astype(bf16)

    @pl.when(pl.program_id(1) == 0)
    def _():
        st_scr[...] = jnp.zeros_like(st_scr)

    l = lbl_ref[...]
    e = jnp.exp(l - jnp.max(l, axis=0, keepdims=True))
    lb = jnp.sum(e[0:layer + 1, :], axis=0, keepdims=True) / jnp.sum(e, axis=0, keepdims=True)
    f_mid = 0.5 * (1.0 + lb)
    f_amp = 0.5 * (1.0 - lb)
    gain = gain_ref[...]

    def forget_block(it, slot):
        rows = pl.ds(pl.multiple_of(it * blk, blk), blk)
        f = f_mid + f_amp * jnp.tanh(0.5 * _dot(h_ref[rows, :], w_scr[...]))
        fg_scr[slot, 0], fg_scr[slot, 1] = _split2(jnp.log(f))
        fg_scr[slot, 2] = (1.0 - f).astype(bf16)

    row = lax.broadcasted_iota(jnp.int32, (c_len, c_len), 0)
    col = lax.broadcasted_iota(jnp.int32, (c_len, c_len), 1)
    causal = row >= col
    tri = jnp.where(causal, 1.0, 0.0).astype(bf16)
    tri2 = jnp.concatenate([tri, tri], axis=1)
    heads = range(HG_HEADS)
    sl = [slice(h * HG_DK, (h + 1) * HG_DK) for h in heads]

    def chunk_operands(it, p, slot):
        rows = pl.ds(pl.multiple_of(it * blk + p * c_len, c_len), c_len)
        in_blk = slice(p * c_len, (p + 1) * c_len)
        cum = _dot(tri2, jnp.concatenate([fg_scr[slot, 0, in_blk, :], fg_scr[slot, 1, in_blk, :]], axis=0))
        k = fg_scr[slot, 2, in_blk, :].astype(f32)
        last = cum[c_len - 1:c_len, :]
        mid = 0.5 * last
        q = q_ref[rows, :].astype(f32)
        return dict(
            rows=rows,
            qt=(q * jnp.exp(jnp.minimum(cum - mid, HG_EXP_CLAMP))).astype(bf16),
            kt=(k * jnp.exp(jnp.minimum(mid - cum, HG_EXP_CLAMP))).astype(bf16),
            qd=(q * jnp.exp(cum)).astype(bf16),
            kd=(k * jnp.exp(last - cum)).astype(bf16),
            e_last=jnp.exp(last),
            v=i_ref[rows, :],
            out_scale=gain * og_ref[rows, :].astype(f32),
        )

    def row_block(it, has_next):
        slot = it % 2
        cs = [chunk_operands(it, p, slot) for p in range(HG_PAR)]
        if has_next:
            forget_block(it + 1, 1 - slot)
        scores =[[jnp.where(causal, _dot_nt(c["qt"][:, sl[h]], c["kt"][:, sl[h]]), 0.0).astype(bf16)
                   for h in heads] for c in cs]
        upd = [[_dot_tn(c["v"][:, sl[h]], c["kd"][:, sl[h]]) for h in heads] for c in cs]
        states = [st_scr[h] for h in heads]
        inter = []
        for p, c in enumerate(cs):
            inter.append([_dot_nt(c["qd"][:, sl[h]], states[h].astype(bf16)) for h in heads])
            states = [states[h] * c["e_last"][:, sl[h]] + upd[p][h] for h in heads]
        for h in heads:
            st_scr[h] = states[h]
        for p, c in enumerate(cs):
            outs = []
            for h in heads:
                o = _dot(scores[p][h], c["v"][:, sl[h]]) + inter[p][h]
                ms = jnp.mean(o * o, axis=-1, keepdims=True)
                outs.append((o * lax.rsqrt(ms + EPS) * c["out_scale"][:, sl[h]]).astype(o_ref.dtype))
            o_ref[c["rows"], :] = jnp.concatenate(outs, axis=1)

    def body(it, carry):
        row_block(it, True)
        return carry

    forget_block(0, 0)
    lax.fori_loop(0, n_blk - 1, body, 0)
    row_block(n_blk - 1, False)


def _hgrn2(proj, h, w_in, lb_logits, hg_gain, layer, batch, seq):
    t = proj.shape[0]
    n_rt = seq // HG_ROWS

    def grp(group):
        return pl.BlockSpec((HG_ROWS, D_MODEL), lambda b, r: (b * n_rt + r, group))

    return pl.pallas_call(
        functools.partial(_hgrn_kernel, layer=layer),
        grid=(batch, n_rt),
        in_specs=[
            grp(P_HQ), grp(0), grp(P_HI), grp(P_HG),
            pl.BlockSpec((D_MODEL, D_MODEL), lambda b, r: (0, G_HF), pipeline_mode=pl.Buffered(1)),
            pl.BlockSpec(lb_logits.shape, lambda b, r: (0, 0)),
            pl.BlockSpec((1, D_MODEL), lambda b, r: (0, 0)),
        ],
        out_specs=grp(0),
        out_shape=jax.ShapeDtypeStruct((t, HG_HEADS * HG_DV), bf16),
        scratch_shapes=[
            pltpu.VMEM((HG_HEADS, HG_DV, HG_DK), f32),
            pltpu.VMEM((D_MODEL, D_MODEL), bf16),
            pltpu.VMEM((2, 3, HG_PAR * HG_CHUNK, D_MODEL), bf16),
        ],
        compiler_params=pltpu.CompilerParams(
            dimension_semantics=("arbitrary", "arbitrary"), vmem_limit_bytes=VMEM_LIMIT),
        name="hgrn2",
    )(proj, h, proj, proj, w_in, lb_logits, hg_gain.reshape(1, HG_HEADS * HG_DV))


def _pair_rms(blocks, gain, group_mean):
    ms = [_dot((x * x).astype(bf16), group_mean) for x in blocks]
    return [x * lax.rsqrt(m + EPS) * gain for x, m in zip(blocks, ms)]


def _softplus2(z):
    return jnp.maximum(z, 0.0) + jnp.log2(1.0 + jnp.exp2(-jnp.abs(z)))


def _sb_kernel(q_ref, k_ref, v_ref, qg_ref, kg_ref, o_ref, kn_scr, q2_scr, acc_scr, car_scr):
    step = pl.program_id(2)
    seq = k_ref.shape[0]
    n_sub = SB_QROWS // SB_T
    lane = lax.broadcasted_iota(jnp.int32, (1, LANES), 1)
    lo_mask = lane < SB_DH

    gr = lax.broadcasted_iota(jnp.int32, (LANES, LANES), 0) // SB_DH
    gc = lax.broadcasted_iota(jnp.int32, (LANES, LANES), 1) // SB_DH
    group_mean = jnp.where(gr == gc, 1.0 / SB_DH, 0.0).astype(bf16)

    @pl.when(step == 0)
    def _():
        prep = 512

        blocks = [k_ref[r * prep:(r + 1) * prep, :].astype(f32) for r in range(seq // prep)]
        for r, kn in enumerate(_pair_rms(blocks, kg_ref[...], group_mean)):
            kn_scr[r * prep:(r + 1) * prep, :] = kn.astype(bf16)

    q_scale = math.log2(math.e) / math.sqrt(SB_DH)
    qn = _pair_rms([q_ref[c * SB_T:(c + 1) * SB_T, :].astype(f32) for c in range(n_sub)],
                   qg_ref[...] * q_scale, group_mean)
    q2 = [jnp.concatenate([jnp.where(lo_mask, q, 0.0).astype(bf16),
                           jnp.where(lo_mask, 0.0, q).astype(bf16)], axis=0) for q in qn]

    row = lax.broadcasted_iota(jnp.int32, (2 * SB_T, SB_T), 0) % SB_T
    col = lax.broadcasted_iota(jnp.int32, (2 * SB_T, SB_T), 1)
    earlier = col < row
    urow = lax.broadcasted_iota(jnp.int32, (2 * SB_T, 2 * SB_T), 0)
    ucol = lax.broadcasted_iota(jnp.int32, (2 * SB_T, 2 * SB_T), 1)
    suffix2 = jnp.where((urow >= ucol) | (ucol >= SB_T), 1.0, 0.0).astype(bf16)[:SB_T]
    suffix_pair = jnp.where((urow >= ucol) & ((urow < SB_T) == (ucol < SB_T)) | (urow >= SB_T) & (ucol < SB_T),
                            1.0, 0.0).astype(bf16)

    first = step == 0
    k_start = [jnp.where(first, 0, step * n_sub - 1) * SB_T] + [
        (step * n_sub + c - 1) * SB_T for c in range(1, n_sub)]
    pair_keys = [pl.ds(pl.multiple_of(k0, SB_T), 2 * SB_T) for k0 in k_start]

    zs = [_dot_nt(q2[c], kn_scr[pair_keys[c], :]) for c in range(n_sub)]
    off = jnp.where(first, 0, SB_T)
    zl = [zs[c][:, :SB_T] for c in range(n_sub)]
    zd = [jnp.where(col + SB_T < row + off if c == 0 else earlier, zs[c][:, SB_T:], NEG_INF)
          for c in range(n_sub)]
    zl[0] = jnp.where(col < row + off, zl[0], NEG_INF)
    zs = [jnp.concatenate([zl[c], zd[c]], axis=1) for c in range(n_sub)]
    rs = [_dot(_softplus2(zs[c]).astype(bf16), suffix_pair) for c in range(n_sub)]

    def still_live(car, next_tile):
        m = jnp.min(jnp.min(car, axis=1, keepdims=True), axis=0, keepdims=True)
        return ((next_tile >= 0) & (m[0, 0] <= SB_DONE_LOG2)).astype(jnp.int32)

    accs, cars, lives = [], [], []
    for c in range(n_sub):
        a = jnp.exp2(zs[c] - rs[c])
        accs.append(_dot(a.astype(bf16), v_ref[pair_keys[c], :]))
        total = rs[c][:, 0:1]
        cars.append(jnp.broadcast_to(total, (2 * SB_T, SB_T)))
        lives.append(still_live(total, step * n_sub + c - 2))
    for c in range(n_sub):
        q2_scr[c] = q2[c]
        acc_scr[c] = accs[c]
        car_scr[c] = cars[c]

    def one_more_tile(c, d):
        kb = step * n_sub + c - d
        keys = pl.ds(pl.multiple_of(kb * SB_T, SB_T), SB_T)
        z = _dot_nt(q2_scr[c], kn_scr[keys, :])
        full = _dot(_softplus2(z).astype(bf16), suffix2)
        car = car_scr[c]
        a = jnp.exp2(z - (full[:, :SB_T] + car))
        acc_scr[c] += _dot(a.astype(bf16), v_ref[keys, :])
        car = car + full[:, SB_T:]
        car_scr[c] = car
        return still_live(car, kb - 1)

    def more(state):
        live = state[1]
        for flag in state[2:]:
            live = live | flag
        return live > 0

    def walk(state):
        d = state[0]
        flags = [lax.cond(state[1 + c] > 0, functools.partial(one_more_tile, c, d), lambda: jnp.int32(0))
                 for c in range(n_sub)]
        return (d + 1, *flags)

    lax.while_loop(more, walk, (jnp.int32(2), *lives))

    for c in range(n_sub):
        o_ref[c * SB_T:(c + 1) * SB_T, :] = jnp.where(
            lo_mask, acc_scr[c, :SB_T, :], acc_scr[c, SB_T:, :]).astype(o_ref.dtype)


def _sb_attn(proj, q_gain, k_gain, batch, seq):
    t = proj.shape[0]
    hpg = D_MODEL // LANES
    n_pairs = SB_HEADS // 2
    n_qt = seq // SB_QROWS
    n_sub = SB_QROWS // SB_T
    return pl.pallas_call(
        _sb_kernel,
        grid=(batch, n_pairs, n_qt),
        in_specs=[
            pl.BlockSpec((SB_QROWS, LANES), lambda b, p, i: (b * n_qt + i, P_SQ * hpg + p)),
            pl.BlockSpec((seq, LANES), lambda b, p, i: (b, P_SK * hpg + p)),
            pl.BlockSpec((seq, LANES), lambda b, p, i: (b, P_SV * hpg + p)),
            pl.BlockSpec((None, 1, LANES), lambda b, p, i: (p, 0, 0)),
            pl.BlockSpec((None, 1, LANES), lambda b, p, i: (p, 0, 0)),
        ],
        out_specs=pl.BlockSpec((SB_QROWS, LANES), lambda b, p, i: (b * n_qt + i, p)),
        out_shape=jax.ShapeDtypeStruct((t, SB_HEADS * SB_DH), bf16),
        scratch_shapes=[
            pltpu.VMEM((seq, LANES), bf16),
            pltpu.VMEM((n_sub, 2 * SB_T, LANES), bf16),
            pltpu.VMEM((n_sub, 2 * SB_T, LANES), f32),
            pltpu.VMEM((n_sub, 2 * SB_T, LANES), f32),
        ],
        compiler_params=pltpu.CompilerParams(
            dimension_semantics=("arbitrary", "arbitrary", "arbitrary"), vmem_limit_bytes=VMEM_LIMIT),
        name="sb_attn",
    )(proj, proj, proj, q_gain.reshape(n_pairs, 1, LANES), k_gain.reshape(n_pairs, 1, LANES))


def _merge_kernel(x_ref, ohg_ref, osb_ref, ga_ref, gb_ref, whg_ref, wsb_ref, wo_ref, o_ref):
    half = x_ref.shape[0] // 2
    parts = [slice(0, half), slice(half, 2 * half)]
    y_hg = [_dot(ohg_ref[r, :], whg_ref[...]) for r in parts]
    y_sb = [_dot(osb_ref[r, :], wsb_ref[...]) for r in parts]
    for r, yh, ys in zip(parts, y_hg, y_sb):
        mixed = (_sigmoid_tanh(ga_ref[r, :].astype(f32)) * yh
                 + _sigmoid_tanh(gb_ref[r, :].astype(f32)) * ys)
        o_ref[r, :] = x_ref[r, :] + _dot(mixed.astype(bf16), wo_ref[...])


def _merge(x2, o_hg, o_sb, proj, w_hg, w_sb, w_o, tm=1024):
    t = x2.shape[0]
    rows = lambda i: (i, 0)
    resident = dict(pipeline_mode=pl.Buffered(1))
    return pl.pallas_call(
        _merge_kernel,
        grid=(t // tm,),
        in_specs=[
            pl.BlockSpec((tm, D_MODEL), rows),
            pl.BlockSpec((tm, D_MODEL), rows),
            pl.BlockSpec((tm, D_MODEL), rows),
            pl.BlockSpec((tm, D_MODEL), lambda i: (i, P_GA)),
            pl.BlockSpec((tm, D_MODEL), lambda i: (i, P_GB)),
            pl.BlockSpec((D_MODEL, D_MODEL), lambda i: (0, 0), **resident),
            pl.BlockSpec((D_MODEL, D_MODEL), lambda i: (0, 0), **resident),
            pl.BlockSpec((D_MODEL, D_MODEL), lambda i: (0, 0), **resident),
        ],
        out_specs=pl.BlockSpec((tm, D_MODEL), rows),
        out_shape=jax.ShapeDtypeStruct((t, D_MODEL), f32),
        compiler_params=pltpu.CompilerParams(
            dimension_semantics=("arbitrary",), vmem_limit_bytes=VMEM_LIMIT),
        name="merge",
    )(x2, o_hg, o_sb, proj, proj, w_hg, w_sb, w_o)


def _ffn_kernel(x_ref, gain_ref, wi_ref, wd_ref, o_ref, h_scr, act_scr):
    def swiglu(h, rows, j):
        cols = slice(j * FF_CHUNK, (j + 1) * FF_CHUNK)
        up_cols = slice(FF_HIDDEN + j * FF_CHUNK, FF_HIDDEN + (j + 1) * FF_CHUNK)
        gate = _dot(h, wi_ref[:, cols])
        up = _dot(h, wi_ref[:, up_cols])
        act_scr[rows, cols] = (gate * _sigmoid(gate) * up).astype(bf16)

    n_blk = 4
    blk = x_ref.shape[0] // n_blk
    for b in range(n_blk):
        rows = slice(b * blk, (b + 1) * blk)
        h = _rms_norm_bf16(x_ref[rows, :], gain_ref[...])
        h_scr[rows, :] = h
        swiglu(h, rows, 0)
    for j in range(1, FF_HIDDEN // FF_CHUNK):
        swiglu(h_scr[...], slice(None), j)
    o_ref[...] = x_ref[...] + _dot(act_scr[...], wd_ref[...])


def _ffn(x1, gain, w_in_bf, w_out_bf, tm=1024):
    t = x1.shape[0]
    resident = dict(pipeline_mode=pl.Buffered(1))
    return pl.pallas_call(
        _ffn_kernel,
        grid=(t // tm,),
        in_specs=[
            pl.BlockSpec((tm, D_MODEL), lambda i: (i, 0)),
            pl.BlockSpec((1, D_MODEL), lambda i: (0, 0)),
            pl.BlockSpec((D_MODEL, 2 * FF_HIDDEN), lambda i: (0, 0), **resident),
            pl.BlockSpec((FF_HIDDEN, D_MODEL), lambda i: (0, 0), **resident),
        ],
        out_specs=pl.BlockSpec((tm, D_MODEL), lambda i: (i, 0)),
        out_shape=jax.ShapeDtypeStruct((t, D_MODEL), f32),
        scratch_shapes=[pltpu.VMEM((tm, D_MODEL), bf16), pltpu.VMEM((tm, FF_HIDDEN), bf16)],
        compiler_params=pltpu.CompilerParams(
            dimension_semantics=("arbitrary",), vmem_limit_bytes=VMEM_LIMIT),
        name="ffn",
    )(x1, gain, w_in_bf, w_out_bf)


def kernel(x, norm1_gain, w_in, lb_logits, hg_out_norm, sb_q_norm, sb_k_norm,
           w_hg_out, w_sb_out, w_o, norm2_gain, w_ffn_in, w_ffn_out):
    batch, seq, d = x.shape
    depth = norm1_gain.shape[0]
    x2 = x.reshape(batch * seq, d)
    for layer in range(depth):
        gain1 = norm1_gain[layer][None, :]
        proj, h = _in_proj(x2, gain1, w_in[layer])
        o_hg = _hgrn2(proj, h, w_in[layer], lb_logits, hg_out_norm[layer], layer, batch, seq)
        o_sb = _sb_attn(proj, sb_q_norm[layer], sb_k_norm[layer], batch, seq)
        x1 = _merge(x2, o_hg, o_sb, proj, w_hg_out[layer].astype(bf16),
                    w_sb_out[layer].astype(bf16), w_o[layer].astype(bf16))
        x2 = _ffn(x1, norm2_gain[layer][None, :], w_ffn_in[layer].astype(bf16),
                  w_ffn_out[layer].astype(bf16))
    return x2.reshape(batch, seq, d)
```

```python
import functools
import math

import jax
import jax.numpy as jnp
from jax import lax
from jax.experimental import pallas as pl
from jax.experimental.pallas import tpu as pltpu

D_MODEL = 1024
HG_HEADS = 8
HG_DK = 128
HG_DV = 128
SB_HEADS = 16
SB_DH = 64
FF_HIDDEN = 2816
FF_CHUNK = 256
EPS = 1e-6
G_HQ, G_HF, G_HI, G_HG, G_SQ, G_SK, G_SV, G_GA, G_GB = range(9)
P_SOURCE = (G_HQ, G_HI, G_HG, G_SQ, G_SK, G_SV, G_GA, G_GB)
P_HQ, P_HI, P_HG, P_SQ, P_SK, P_SV, P_GA, P_GB = range(8)

LANES = 128
HG_CHUNK = 128
HG_PAR = 4
HG_ROWS = 1024
HG_EXP_CLAMP = 80.0
SB_T = 128
SB_QROWS = 2048
SB_DONE_LOG2 = 128.0
VMEM_LIMIT = 56 * 1024 * 1024

f32 = jnp.float32
bf16 = jnp.bfloat16
NEG_INF = float("-inf")


def _sigmoid(x):
    return 1.0 / (1.0 + jnp.exp(-x))


def _sigmoid_tanh(x):
    return 0.5 * jnp.tanh(0.5 * x) + 0.5


def _dot(a, b):
    return jnp.dot(a, b, preferred_element_type=f32)


def _dot_nt(a, b):
    return lax.dot_general(a, b, (((1,), (1,)), ((), ())), preferred_element_type=f32)


def _dot_tn(a, b):
    return lax.dot_general(a, b, (((0,), (0,)), ((), ())), preferred_element_type=f32)


def _split2(x):
    hi = x.astype(bf16)
    lo = (x - hi.astype(f32)).astype(bf16)
    return hi, lo


def _rms_norm_bf16(x, gain):
    ms = jnp.mean(x * x, axis=-1, keepdims=True)
    return (x * lax.rsqrt(ms + EPS) * gain).astype(bf16)


def _inproj_kernel(x_ref, gain_ref, w_ref, o_ref, h_ref):
    j = pl.program_id(1)

    @pl.when(j == 0)
    def _():
        w = w_ref[...].astype(bf16)
        n_blk = 4
        blk = x_ref.shape[0] // n_blk
        for b in range(n_blk):
            rows = slice(b * blk, (b + 1) * blk)
            h = _rms_norm_bf16(x_ref[rows, :], gain_ref[...])
            h_ref[rows, :] = h
            o_ref[rows, :] = _dot(h, w).astype(o_ref.dtype)

    is_gate = j == P_HG

    @pl.when(is_gate)
    def _():
        o_ref[...] = _sigmoid_tanh(_dot(h_ref[...], w_ref[...].astype(bf16))).astype(o_ref.dtype)

    @pl.when(jnp.logical_not(is_gate) & (j > 0))
    def _():
        o_ref[...] = _dot(h_ref[...], w_ref[...].astype(bf16)).astype(o_ref.dtype)


def _in_proj(x2, gain, w_in, tm=2048):
    t = x2.shape[0]
    n_groups = len(P_SOURCE)
    return pl.pallas_call(
        _inproj_kernel,
        grid=(t // tm, n_groups),
        in_specs=[
            pl.BlockSpec((tm, D_MODEL), lambda i, j: (i, 0)),
            pl.BlockSpec((1, D_MODEL), lambda i, j: (0, 0)),
            pl.BlockSpec((D_MODEL, D_MODEL), lambda i, j: (0, jnp.where(j >= G_HF, j + 1, j))),
        ],
        out_specs=[
            pl.BlockSpec((tm, D_MODEL), lambda i, j: (i, j)),
            pl.BlockSpec((tm, D_MODEL), lambda i, j: (i, 0)),
        ],
        out_shape=[
            jax.ShapeDtypeStruct((t, n_groups * D_MODEL), bf16),
            jax.ShapeDtypeStruct((t, D_MODEL), bf16),
        ],
        compiler_params=pltpu.CompilerParams(
            dimension_semantics=("arbitrary", "arbitrary"), vmem_limit_bytes=VMEM_LIMIT),
        name="in_proj",
    )(x2, gain, w_in)


def _hgrn_kernel(q_ref, h_ref, i_ref, og_ref, w_ref, lbl_ref, gain_ref, o_ref, st_scr, w_scr, fg_scr,
                 *, layer):
    c_len = HG_CHUNK
    blk = HG_PAR * c_len
    n_blk = q_ref.shape[0] // blk

    @pl.when((pl.program_id(0) == 0) & (pl.program_id(1) == 0))
    def _():
        w_scr[...] = w_ref[...].astype(bf16)

    @pl.when(pl.program_id(1) == 0)
    def _():
        st_scr[...] = jnp.zeros_like(st_scr)

    l = lbl_ref[...]
    e = jnp.exp(l - jnp.max(l, axis=0, keepdims=True))
    lb = jnp.sum(e[0:layer + 1, :], axis=0, keepdims=True) / jnp.sum(e, axis=0, keepdims=True)
    f_mid = 0.5 * (1.0 + lb)
    f_amp = 0.5 * (1.0 - lb)
    gain = gain_ref[...]

    def forget_block(it, slot):
        rows = pl.ds(pl.multiple_of(it * blk, blk), blk)
        f = f_mid + f_amp * jnp.tanh(0.5 * _dot(h_ref[rows, :], w_scr[...]))
        fg_scr[slot, 0], fg_scr[slot, 1] = _split2(jnp.log(f))
        fg_scr[slot, 2] = (1.0 - f).astype(bf16)

    row = lax.broadcasted_iota(jnp.int32, (c_len, c_len), 0)
    col = lax.broadcasted_iota(jnp.int32, (c_len, c_len), 1)
    causal = row >= col
    tri = jnp.where(causal, 1.0, 0.0).astype(bf16)
    tri2 = jnp.concatenate([tri, tri], axis=1)
    heads = range(HG_HEADS)
    sl = [slice(h * HG_DK, (h + 1) * HG_DK) for h in heads]

    def chunk_operands(it, p, slot):
        rows = pl.ds(pl.multiple_of(it * blk + p * c_len, c_len), c_len)
        in_blk = slice(p * c_len, (p + 1) * c_len)
        cum = _dot(tri2, jnp.concatenate([fg_scr[slot, 0, in_blk, :], fg_scr[slot, 1, in_blk, :]], axis=0))
        k = fg_scr[slot, 2, in_blk, :].astype(f32)
        last = cum[c_len - 1:c_len, :]
        mid = 0.5 * last
        q = q_ref[rows, :].astype(f32)
        return dict(
            rows=rows,
            qt=(q * jnp.exp(jnp.minimum(cum - mid, HG_EXP_CLAMP))).astype(bf16),
            kt=(k * jnp.exp(jnp.minimum(mid - cum, HG_EXP_CLAMP))).astype(bf16),
            qd=(q * jnp.exp(cum)).astype(bf16),
            kd=(k * jnp.exp(last - cum)).astype(bf16),
            e_last=jnp.exp(last),
            v=i_ref[rows, :],
            out_scale=gain * og_ref[rows, :].astype(f32),
        )

    def row_block(it, has_next):
        slot = it % 2
        cs = [chunk_operands(it, p, slot) for p in range(HG_PAR)]
        if has_next:
            forget_block(it + 1, 1 - slot)
        scores =[[jnp.where(causal, _dot_nt(c["qt"][:, sl[h]], c["kt"][:, sl[h]]), 0.0).astype(bf16)
                   for h in heads] for c in cs]
        upd = [[_dot_tn(c["v"][:, sl[h]], c["kd"][:, sl[h]]) for h in heads] for c in cs]
        states = [st_scr[h] for h in heads]
        inter = []
        for p, c in enumerate(cs):
            inter.append([_dot_nt(c["qd"][:, sl[h]], states[h].astype(bf16)) for h in heads])
            states = [states[h] * c["e_last"][:, sl[h]] + upd[p][h] for h in heads]
        for h in heads:
            st_scr[h] = states[h]
        for p, c in enumerate(cs):
            outs = []
            for h in heads:
                o = _dot(scores[p][h], c["v"][:, sl[h]]) + inter[p][h]
                ms = jnp.mean(o * o, axis=-1, keepdims=True)
                outs.append((o * lax.rsqrt(ms + EPS) * c["out_scale"][:, sl[h]]).astype(o_ref.dtype))
            o_ref[c["rows"], :] = jnp.concatenate(outs, axis=1)

    def body(it, carry):
        row_block(it, True)
        return carry

    forget_block(0, 0)
    lax.fori_loop(0, n_blk - 1, body, 0)
    row_block(n_blk - 1, False)


def _hgrn2(proj, h, w_in, lb_logits, hg_gain, layer, batch, seq):
    t = proj.shape[0]
    n_rt = seq // HG_ROWS

    def grp(group):
        return pl.BlockSpec((HG_ROWS, D_MODEL), lambda b, r: (b * n_rt + r, group))

    return pl.pallas_call(
        functools.partial(_hgrn_kernel, layer=layer),
        grid=(batch, n_rt),
        in_specs=[
            grp(P_HQ), grp(0), grp(P_HI), grp(P_HG),
            pl.BlockSpec((D_MODEL, D_MODEL), lambda b, r: (0, G_HF), pipeline_mode=pl.Buffered(1)),
            pl.BlockSpec(lb_logits.shape, lambda b, r: (0, 0)),
            pl.BlockSpec((1, D_MODEL), lambda b, r: (0, 0)),
        ],
        out_specs=grp(0),
        out_shape=jax.ShapeDtypeStruct((t, HG_HEADS * HG_DV), bf16),
        scratch_shapes=[
            pltpu.VMEM((HG_HEADS, HG_DV, HG_DK), f32),
            pltpu.VMEM((D_MODEL, D_MODEL), bf16),
            pltpu.VMEM((2, 3, HG_PAR * HG_CHUNK, D_MODEL), bf16),
        ],
        compiler_params=pltpu.CompilerParams(
            dimension_semantics=("arbitrary", "arbitrary"), vmem_limit_bytes=VMEM_LIMIT),
        name="hgrn2",
    )(proj, h, proj, proj, w_in, lb_logits, hg_gain.reshape(1, HG_HEADS * HG_DV))


def _pair_rms(blocks, gain, group_mean):
    ms = [_dot((x * x).astype(bf16), group_mean) for x in blocks]
    return [x * lax.rsqrt(m + EPS) * gain for x, m in zip(blocks, ms)]


def _softplus2(z):
    return jnp.maximum(z, 0.0) + jnp.log2(1.0 + jnp.exp2(-jnp.abs(z)))


def _sb_kernel(q_ref, k_ref, v_ref, qg_ref, kg_ref, o_ref, kn_scr, q2_scr, acc_scr, car_scr):
    step = pl.program_id(2)
    seq = k_ref.shape[0]
    n_sub = SB_QROWS // SB_T
    lane = lax.broadcasted_iota(jnp.int32, (1, LANES), 1)
    lo_mask = lane < SB_DH

    gr = lax.broadcasted_iota(jnp.int32, (LANES, LANES), 0) // SB_DH
    gc = lax.broadcasted_iota(jnp.int32, (LANES, LANES), 1) // SB_DH
    group_mean = jnp.where(gr == gc, 1.0 / SB_DH, 0.0).astype(bf16)

    @pl.when(step == 0)
    def _():
        prep = 512

        blocks = [k_ref[r * prep:(r + 1) * prep, :].astype(f32) for r in range(seq // prep)]
        for r, kn in enumerate(_pair_rms(blocks, kg_ref[...], group_mean)):
            kn_scr[r * prep:(r + 1) * prep, :] = kn.astype(bf16)

    q_scale = math.log2(math.e) / math.sqrt(SB_DH)
    qn = _pair_rms([q_ref[c * SB_T:(c + 1) * SB_T, :].astype(f32) for c in range(n_sub)],
                   qg_ref[...] * q_scale, group_mean)
    q2 = [jnp.concatenate([jnp.where(lo_mask, q, 0.0).astype(bf16),
                           jnp.where(lo_mask, 0.0, q).astype(bf16)], axis=0) for q in qn]

    row = lax.broadcasted_iota(jnp.int32, (2 * SB_T, SB_T), 0) % SB_T
    col = lax.broadcasted_iota(jnp.int32, (2 * SB_T, SB_T), 1)
    earlier = col < row
    urow = lax.broadcasted_iota(jnp.int32, (2 * SB_T, 2 * SB_T), 0)
    ucol = lax.broadcasted_iota(jnp.int32, (2 * SB_T, 2 * SB_T), 1)
    suffix2 = jnp.where((urow >= ucol) | (ucol >= SB_T), 1.0, 0.0).astype(bf16)[:SB_T]
    suffix_pair = jnp.where((urow >= ucol) & ((urow < SB_T) == (ucol < SB_T)) | (urow >= SB_T) & (ucol < SB_T),
                            1.0, 0.0).astype(bf16)

    first = step == 0
    k_start = [jnp.where(first, 0, step * n_sub - 1) * SB_T] + [
        (step * n_sub + c - 1) * SB_T for c in range(1, n_sub)]
    pair_keys = [pl.ds(pl.multiple_of(k0, SB_T), 2 * SB_T) for k0 in k_start]

    zs = [_dot_nt(q2[c], kn_scr[pair_keys[c], :]) for c in range(n_sub)]
    off = jnp.where(first, 0, SB_T)
    zl = [zs[c][:, :SB_T] for c in range(n_sub)]
    zd = [jnp.where(col + SB_T < row + off if c == 0 else earlier, zs[c][:, SB_T:], NEG_INF)
          for c in range(n_sub)]
    zl[0] = jnp.where(col < row + off, zl[0], NEG_INF)
    zs = [jnp.concatenate([zl[c], zd[c]], axis=1) for c in range(n_sub)]
    rs = [_dot(_softplus2(zs[c]).astype(bf16), suffix_pair) for c in range(n_sub)]

    def still_live(car, next_tile):
        m = jnp.min(jnp.min(car, axis=1, keepdims=True), axis=0, keepdims=True)
        return ((next_tile >= 0) & (m[0, 0] <= SB_DONE_LOG2)).astype(jnp.int32)

    accs, cars, lives = [], [], []
    for c in range(n_sub):
        a = jnp.exp2(zs[c] - rs[c])
        accs.append(_dot(a.astype(bf16), v_ref[pair_keys[c], :]))
        total = rs[c][:, 0:1]
        cars.append(jnp.broadcast_to(total, (2 * SB_T, SB_T)))
        lives.append(still_live(total, step * n_sub + c - 2))
    for c in range(n_sub):
        q2_scr[c] = q2[c]
        acc_scr[c] = accs[c]
        car_scr[c] = cars[c]

    def one_more_tile(c, d):
        kb = step * n_sub + c - d
        keys = pl.ds(pl.multiple_of(kb * SB_T, SB_T), SB_T)
        z = _dot_nt(q2_scr[c], kn_scr[keys, :])
        full = _dot(_softplus2(z).astype(bf16), suffix2)
        car = car_scr[c]
        a = jnp.exp2(z - (full[:, :SB_T] + car))
        acc_scr[c] += _dot(a.astype(bf16), v_ref[keys, :])
        car = car + full[:, SB_T:]
        car_scr[c] = car
        return still_live(car, kb - 1)

    def more(state):
        live = state[1]
        for flag in state[2:]:
            live = live | flag
        return live > 0

    def walk(state):
        d = state[0]
        flags = [lax.cond(state[1 + c] > 0, functools.partial(one_more_tile, c, d), lambda: jnp.int32(0))
                 for c in range(n_sub)]
        return (d + 1, *flags)

    lax.while_loop(more, walk, (jnp.int32(2), *lives))

    for c in range(n_sub):
        o_ref[c * SB_T:(c + 1) * SB_T, :] = jnp.where(
            lo_mask, acc_scr[c, :SB_T, :], acc_scr[c, SB_T:, :]).astype(o_ref.dtype)


def _sb_attn(proj, q_gain, k_gain, batch, seq):
    t = proj.shape[0]
    hpg = D_MODEL // LANES
    n_pairs = SB_HEADS // 2
    n_qt = seq // SB_QROWS
    n_sub = SB_QROWS // SB_T
    return pl.pallas_call(
        _sb_kernel,
        grid=(batch, n_pairs, n_qt),
        in_specs=[
            pl.BlockSpec((SB_QROWS, LANES), lambda b, p, i: (b * n_qt + i, P_SQ * hpg + p)),
            pl.BlockSpec((seq, LANES), lambda b, p, i: (b, P_SK * hpg + p)),
            pl.BlockSpec((seq, LANES), lambda b, p, i: (b, P_SV * hpg + p)),
            pl.BlockSpec((None, 1, LANES), lambda b, p, i: (p, 0, 0)),
            pl.BlockSpec((None, 1, LANES), lambda b, p, i: (p, 0, 0)),
        ],
        out_specs=pl.BlockSpec((SB_QROWS, LANES), lambda b, p, i: (b * n_qt + i, p)),
        out_shape=jax.ShapeDtypeStruct((t, SB_HEADS * SB_DH), bf16),
        scratch_shapes=[
            pltpu.VMEM((seq, LANES), bf16),
            pltpu.VMEM((n_sub, 2 * SB_T, LANES), bf16),
            pltpu.VMEM((n_sub, 2 * SB_T, LANES), f32),
            pltpu.VMEM((n_sub, 2 * SB_T, LANES), f32),
        ],
        compiler_params=pltpu.CompilerParams(
            dimension_semantics=("arbitrary", "arbitrary", "arbitrary"), vmem_limit_bytes=VMEM_LIMIT),
        name="sb_attn",
    )(proj, proj, proj, q_gain.reshape(n_pairs, 1, LANES), k_gain.reshape(n_pairs, 1, LANES))


def _merge_kernel(x_ref, ohg_ref, osb_ref, ga_ref, gb_ref, whg_ref, wsb_ref, wo_ref, o_ref):
    half = x_ref.shape[0] // 2
    parts = [slice(0, half), slice(half, 2 * half)]
    y_hg = [_dot(ohg_ref[r, :], whg_ref[...]) for r in parts]
    y_sb = [_dot(osb_ref[r, :], wsb_ref[...]) for r in parts]
    for r, yh, ys in zip(parts, y_hg, y_sb):
        mixed = (_sigmoid_tanh(ga_ref[r, :].astype(f32)) * yh
                 + _sigmoid_tanh(gb_ref[r, :].astype(f32)) * ys)
        o_ref[r, :] = x_ref[r, :] + _dot(mixed.astype(bf16), wo_ref[...])


def _merge(x2, o_hg, o_sb, proj, w_hg, w_sb, w_o, tm=1024):
    t = x2.shape[0]
    rows = lambda i: (i, 0)
    resident = dict(pipeline_mode=pl.Buffered(1))
    return pl.pallas_call(
        _merge_kernel,
        grid=(t // tm,),
        in_specs=[
            pl.BlockSpec((tm, D_MODEL), rows),
            pl.BlockSpec((tm, D_MODEL), rows),
            pl.BlockSpec((tm, D_MODEL), rows),
            pl.BlockSpec((tm, D_MODEL), lambda i: (i, P_GA)),
            pl.BlockSpec((tm, D_MODEL), lambda i: (i, P_GB)),
            pl.BlockSpec((D_MODEL, D_MODEL), lambda i: (0, 0), **resident),
            pl.BlockSpec((D_MODEL, D_MODEL), lambda i: (0, 0), **resident),
            pl.BlockSpec((D_MODEL, D_MODEL), lambda i: (0, 0), **resident),
        ],
        out_specs=pl.BlockSpec((tm, D_MODEL), rows),
        out_shape=jax.ShapeDtypeStruct((t, D_MODEL), f32),
        compiler_params=pltpu.CompilerParams(
            dimension_semantics=("arbitrary",), vmem_limit_bytes=VMEM_LIMIT),
        name="merge",
    )(x2, o_hg, o_sb, proj, proj, w_hg, w_sb, w_o)


def _ffn_kernel(x_ref, gain_ref, wi_ref, wd_ref, o_ref, h_scr, act_scr):
    def swiglu(h, rows, j):
        cols = slice(j * FF_CHUNK, (j + 1) * FF_CHUNK)
        up_cols = slice(FF_HIDDEN + j * FF_CHUNK, FF_HIDDEN + (j + 1) * FF_CHUNK)
        gate = _dot(h, wi_ref[:, cols])
        up = _dot(h, wi_ref[:, up_cols])
        act_scr[rows, cols] = (gate * _sigmoid(gate) * up).astype(bf16)

    n_blk = 4
    blk = x_ref.shape[0] // n_blk
    for b in range(n_blk):
        rows = slice(b * blk, (b + 1) * blk)
        h = _rms_norm_bf16(x_ref[rows, :], gain_ref[...])
        h_scr[rows, :] = h
        swiglu(h, rows, 0)
    for j in range(1, FF_HIDDEN // FF_CHUNK):
        swiglu(h_scr[...], slice(None), j)
    o_ref[...] = x_ref[...] + _dot(act_scr[...], wd_ref[...])


def _ffn(x1, gain, w_in_bf, w_out_bf, tm=1024):
    t = x1.shape[0]
    resident = dict(pipeline_mode=pl.Buffered(1))
    return pl.pallas_call(
        _ffn_kernel,
        grid=(t // tm,),
        in_specs=[
            pl.BlockSpec((tm, D_MODEL), lambda i: (i, 0)),
            pl.BlockSpec((1, D_MODEL), lambda i: (0, 0)),
            pl.BlockSpec((D_MODEL, 2 * FF_HIDDEN), lambda i: (0, 0), **resident),
            pl.BlockSpec((FF_HIDDEN, D_MODEL), lambda i: (0, 0), **resident),
        ],
        out_specs=pl.BlockSpec((tm, D_MODEL), lambda i: (i, 0)),
        out_shape=jax.ShapeDtypeStruct((t, D_MODEL), f32),
        scratch_shapes=[pltpu.VMEM((tm, D_MODEL), bf16), pltpu.VMEM((tm, FF_HIDDEN), bf16)],
        compiler_params=pltpu.CompilerParams(
            dimension_semantics=("arbitrary",), vmem_limit_bytes=VMEM_LIMIT),
        name="ffn",
    )(x1, gain, w_in_bf, w_out_bf)


def kernel(x, norm1_gain, w_in, lb_logits, hg_out_norm, sb_q_norm, sb_k_norm,
           w_hg_out, w_sb_out, w_o, norm2_gain, w_ffn_in, w_ffn_out):
    batch, seq, d = x.shape
    depth = norm1_gain.shape[0]
    x2 = x.reshape(batch * seq, d)
    for layer in range(depth):
        gain1 = norm1_gain[layer][None, :]
        proj, h = _in_proj(x2, gain1, w_in[layer])
        o_hg = _hgrn2(proj, h, w_in[layer], lb_logits, hg_out_norm[layer], layer, batch, seq)
        o_sb = _sb_attn(proj, sb_q_norm[layer], sb_k_norm[layer], batch, seq)
        x1 = _merge(x2, o_hg, o_sb, proj, w_hg_out[layer].astype(bf16),
                    w_sb_out[layer].astype(bf16), w_o[layer].astype(bf16))
        x2 = _ffn(x1, norm2_gain[layer][None, :], w_ffn_in[layer].astype(bf16),
                  w_ffn_out[layer].astype(bf16))
    return x2.reshape(batch, seq, d)
```

```python
import functools
import math

import jax
import jax.numpy as jnp
from jax import lax
from jax.experimental import pallas as pl
from jax.experimental.pallas import tpu as pltpu

D_MODEL = 1024
HG_HEADS = 8
HG_DK = 128
HG_DV = 128
SB_HEADS = 16
SB_DH = 64
FF_HIDDEN = 2816
FF_CHUNK = 256
EPS = 1e-6
G_HQ, G_HF, G_HI, G_HG, G_SQ, G_SK, G_SV, G_GA, G_GB = range(9)
P_SOURCE = (G_HQ, G_HI, G_HG, G_SQ, G_SK, G_SV, G_GA, G_GB)
P_HQ, P_HI, P_HG, P_SQ, P_SK, P_SV, P_GA, P_GB = range(8)

LANES = 128
HG_CHUNK = 128
HG_PAR = 4
HG_ROWS = 1024
HG_EXP_CLAMP = 80.0
SB_T = 128
SB_QROWS = 2048
SB_DONE_LOG2 = 128.0
VMEM_LIMIT = 56 * 1024 * 1024

f32 = jnp.float32
bf16 = jnp.bfloat16
NEG_INF = float("-inf")


def _sigmoid(x):
    return 1.0 / (1.0 + jnp.exp(-x))


def _sigmoid_tanh(x):
    return 0.5 * jnp.tanh(0.5 * x) + 0.5


def _dot(a, b):
    return jnp.dot(a, b, preferred_element_type=f32)


def _dot_nt(a, b):
    return lax.dot_general(a, b, (((1,), (1,)), ((), ())), preferred_element_type=f32)


def _dot_tn(a, b):
    return lax.dot_general(a, b, (((0,), (0,)), ((), ())), preferred_element_type=f32)


def _split2(x):
    hi = x.astype(bf16)
    lo = (x - hi.astype(f32)).astype(bf16)
    return hi, lo


def _rms_norm_bf16(x, gain):
    ms = jnp.mean(x * x, axis=-1, keepdims=True)
    return (x * lax.rsqrt(ms + EPS) * gain).astype(bf16)


def _inproj_kernel(x_lo_ref, x_hi_ref, gain_ref, w_ref, o_ref, h_ref):
    j = pl.program_id(1)

    @pl.when(j == 0)
    def _():
        w = w_ref[...].astype(bf16)
        half = x_lo_ref.shape[0]
        blk = half // 2
        for b, x_ref in enumerate((x_lo_ref, x_lo_ref, x_hi_ref, x_hi_ref)):
            src = slice((b % 2) * blk, (b % 2 + 1) * blk)
            rows = slice(b * blk, (b + 1) * blk)
            h = _rms_norm_bf16(x_ref[src, :], gain_ref[...])
            h_ref[rows, :] = h
            o_ref[rows, :] = _dot(h, w).astype(o_ref.dtype)

    is_gate = j == P_HG

    @pl.when(is_gate)
    def _():
        o_ref[...] = _sigmoid_tanh(_dot(h_ref[...], w_ref[...].astype(bf16))).astype(o_ref.dtype)

    @pl.when(jnp.logical_not(is_gate) & (j > 0))
    def _():
        o_ref[...] = _dot(h_ref[...], w_ref[...].astype(bf16)).astype(o_ref.dtype)


def _in_proj(x2, gain, w_in, tm=2048):
    t = x2.shape[0]
    n_groups = len(P_SOURCE)
    n_tiles = t // tm

    def x_half(which, from_group):
        def index(i, j):
            tile = jnp.where(j >= from_group, jnp.minimum(i + 1, n_tiles - 1), i)
            return (2 * tile + which, 0)
        return pl.BlockSpec((tm // 2, D_MODEL), index)

    return pl.pallas_call(
        _inproj_kernel,
        grid=(n_tiles, n_groups),
        in_specs=[
            x_half(0, 2),
            x_half(1, 4),
            pl.BlockSpec((1, D_MODEL), lambda i, j: (0, 0)),
            pl.BlockSpec((D_MODEL, D_MODEL), lambda i, j: (0, jnp.where(j >= G_HF, j + 1, j))),
        ],
        out_specs=[
            pl.BlockSpec((tm, D_MODEL), lambda i, j: (i, j)),
            pl.BlockSpec((tm, D_MODEL), lambda i, j: (i, 0)),
        ],
        out_shape=[
            jax.ShapeDtypeStruct((t, n_groups * D_MODEL), bf16),
            jax.ShapeDtypeStruct((t, D_MODEL), bf16),
        ],
        compiler_params=pltpu.CompilerParams(
            dimension_semantics=("arbitrary", "arbitrary"), vmem_limit_bytes=VMEM_LIMIT),
        name="in_proj",
    )(x2, x2, gain, w_in)


def _hgrn_kernel(q_ref, h_ref, i_ref, og_ref, w_ref, lbl_ref, gain_ref, o_ref, st_scr, w_scr, fg_scr,
                 *, layer):
    c_len = HG_CHUNK
    blk = HG_PAR * c_len
    n_blk = q_ref.shape[0] // blk

    @pl.when((pl.program_id(0) == 0) & (pl.program_id(1) == 0))
    def _():
        w_scr[...] = w_ref[...].astype(bf16)

    @pl.when(pl.program_id(1) == 0)
    def _():
        st_scr[...] = jnp.zeros_like(st_scr)

    l = lbl_ref[...]
    e = jnp.exp(l - jnp.max(l, axis=0, keepdims=True))
    lb = jnp.sum(e[0:layer + 1, :], axis=0, keepdims=True) / jnp.sum(e, axis=0, keepdims=True)
    f_mid = 0.5 * (1.0 + lb)
    f_amp = 0.5 * (1.0 - lb)
    gain = gain_ref[...]

    def forget_block(it, slot):
        rows = pl.ds(pl.multiple_of(it * blk, blk), blk)
        f = f_mid + f_amp * jnp.tanh(0.5 * _dot(h_ref[rows, :], w_scr[...]))
        fg_scr[slot, 0], fg_scr[slot, 1] = _split2(jnp.log(f))
        fg_scr[slot, 2] = (1.0 - f).astype(bf16)

    row = lax.broadcasted_iota(jnp.int32, (c_len, c_len), 0)
    col = lax.broadcasted_iota(jnp.int32, (c_len, c_len), 1)
    causal = row >= col
    tri = jnp.where(causal, 1.0, 0.0).astype(bf16)
    tri2 = jnp.concatenate([tri, tri], axis=1)
    heads = range(HG_HEADS)
    sl = [slice(h * HG_DK, (h + 1) * HG_DK) for h in heads]

    def chunk_operands(it, p, slot):
        rows = pl.ds(pl.multiple_of(it * blk + p * c_len, c_len), c_len)
        in_blk = slice(p * c_len, (p + 1) * c_len)
        cum = _dot(tri2, jnp.concatenate([fg_scr[slot, 0, in_blk, :], fg_scr[slot, 1, in_blk, :]], axis=0))
        k = fg_scr[slot, 2, in_blk, :].astype(f32)
        last = cum[c_len - 1:c_len, :]
        mid = 0.5 * last
        q = q_ref[rows, :].astype(f32)
        return dict(
            rows=rows,
            qt=(q * jnp.exp(jnp.minimum(cum - mid, HG_EXP_CLAMP))).astype(bf16),
            kt=(k * jnp.exp(jnp.minimum(mid - cum, HG_EXP_CLAMP))).astype(bf16),
            qd=(q * jnp.exp(cum)).astype(bf16),
            kd=(k * jnp.exp(last - cum)).astype(bf16),
            e_last=jnp.exp(last),
            v=i_ref[rows, :],
            out_scale=gain * og_ref[rows, :].astype(f32),
        )

    def row_block(it, has_next):
        slot = it % 2
        cs = [chunk_operands(it, p, slot) for p in range(HG_PAR)]
        if has_next:
            forget_block(it + 1, 1 - slot)
        scores =[[jnp.where(causal, _dot_nt(c["qt"][:, sl[h]], c["kt"][:, sl[h]]), 0.0).astype(bf16)
                   for h in heads] for c in cs]
        upd = [[_dot_tn(c["v"][:, sl[h]], c["kd"][:, sl[h]]) for h in heads] for c in cs]
        states = [st_scr[h] for h in heads]
        inter = []
        for p, c in enumerate(cs):
            inter.append([_dot_nt(c["qd"][:, sl[h]], states[h].astype(bf16)) for h in heads])
            states = [states[h] * c["e_last"][:, sl[h]] + upd[p][h] for h in heads]
        for h in heads:
            st_scr[h] = states[h]
        for p, c in enumerate(cs):
            outs = []
            for h in heads:
                o = _dot(scores[p][h], c["v"][:, sl[h]]) + inter[p][h]
                ms = jnp.mean(o * o, axis=-1, keepdims=True)
                outs.append((o * lax.rsqrt(ms + EPS) * c["out_scale"][:, sl[h]]).astype(o_ref.dtype))
            o_ref[c["rows"], :] = jnp.concatenate(outs, axis=1)

    def body(it, carry):
        row_block(it, True)
        return carry

    forget_block(0, 0)
    lax.fori_loop(0, n_blk - 1, body, 0)
    row_block(n_blk - 1, False)


def _hgrn2(proj, h, w_in, lb_logits, hg_gain, layer, batch, seq):
    t = proj.shape[0]
    n_rt = seq // HG_ROWS

    def grp(group):
        return pl.BlockSpec((HG_ROWS, D_MODEL), lambda b, r: (b * n_rt + r, group))

    return pl.pallas_call(
        functools.partial(_hgrn_kernel, layer=layer),
        grid=(batch, n_rt),
        in_specs=[
            grp(P_HQ), grp(0), grp(P_HI), grp(P_HG),
            pl.BlockSpec((D_MODEL, D_MODEL), lambda b, r: (0, G_HF), pipeline_mode=pl.Buffered(1)),
            pl.BlockSpec(lb_logits.shape, lambda b, r: (0, 0)),
            pl.BlockSpec((1, D_MODEL), lambda b, r: (0, 0)),
        ],
        out_specs=grp(0),
        out_shape=jax.ShapeDtypeStruct((t, HG_HEADS * HG_DV), bf16),
        scratch_shapes=[
            pltpu.VMEM((HG_HEADS, HG_DV, HG_DK), f32),
            pltpu.VMEM((D_MODEL, D_MODEL), bf16),
            pltpu.VMEM((2, 3, HG_PAR * HG_CHUNK, D_MODEL), bf16),
        ],
        compiler_params=pltpu.CompilerParams(
            dimension_semantics=("arbitrary", "arbitrary"), vmem_limit_bytes=VMEM_LIMIT),
        name="hgrn2",
    )(proj, h, proj, proj, w_in, lb_logits, hg_gain.reshape(1, HG_HEADS * HG_DV))


def _pair_rms(blocks, gain, group_mean):
    ms = [_dot((x * x).astype(bf16), group_mean) for x in blocks]
    return [x * lax.rsqrt(m + EPS) * gain for x, m in zip(blocks, ms)]


def _softplus2(z):
    return jnp.maximum(z, 0.0) + jnp.log2(1.0 + jnp.exp2(-jnp.abs(z)))


def _sb_kernel(q_ref, k_ref, v_ref, qg_ref, kg_ref, o_ref, kn_scr, q2_scr, acc_scr, car_scr):
    step = pl.program_id(2)
    seq = k_ref.shape[0]
    n_sub = SB_QROWS // SB_T
    lane = lax.broadcasted_iota(jnp.int32, (1, LANES), 1)
    lo_mask = lane < SB_DH

    gr = lax.broadcasted_iota(jnp.int32, (LANES, LANES), 0) // SB_DH
    gc = lax.broadcasted_iota(jnp.int32, (LANES, LANES), 1) // SB_DH
    group_mean = jnp.where(gr == gc, 1.0 / SB_DH, 0.0).astype(bf16)

    @pl.when(step == 0)
    def _():
        prep = 512

        blocks = [k_ref[r * prep:(r + 1) * prep, :].astype(f32) for r in range(seq // prep)]
        for r, kn in enumerate(_pair_rms(blocks, kg_ref[...], group_mean)):
            kn_scr[r * prep:(r + 1) * prep, :] = kn.astype(bf16)

    q_scale = math.log2(math.e) / math.sqrt(SB_DH)
    qn = _pair_rms([q_ref[c * SB_T:(c + 1) * SB_T, :].astype(f32) for c in range(n_sub)],
                   qg_ref[...] * q_scale, group_mean)
    q2 = [jnp.concatenate([jnp.where(lo_mask, q, 0.0).astype(bf16),
                           jnp.where(lo_mask, 0.0, q).astype(bf16)], axis=0) for q in qn]

    row = lax.broadcasted_iota(jnp.int32, (2 * SB_T, SB_T), 0) % SB_T
    col = lax.broadcasted_iota(jnp.int32, (2 * SB_T, SB_T), 1)
    earlier = col < row
    urow = lax.broadcasted_iota(jnp.int32, (2 * SB_T, 2 * SB_T), 0)
    ucol = lax.broadcasted_iota(jnp.int32, (2 * SB_T, 2 * SB_T), 1)
    suffix2 = jnp.where((urow >= ucol) | (ucol >= SB_T), 1.0, 0.0).astype(bf16)[:SB_T]
    suffix_pair = jnp.where((urow >= ucol) & ((urow < SB_T) == (ucol < SB_T)) | (urow >= SB_T) & (ucol < SB_T),
                            1.0, 0.0).astype(bf16)

    first = step == 0
    k_start = [jnp.where(first, 0, step * n_sub - 1) * SB_T] + [
        (step * n_sub + c - 1) * SB_T for c in range(1, n_sub)]
    pair_keys = [pl.ds(pl.multiple_of(k0, SB_T), 2 * SB_T) for k0 in k_start]

    zs = [_dot_nt(q2[c], kn_scr[pair_keys[c], :]) for c in range(n_sub)]
    off = jnp.where(first, 0, SB_T)
    zl = [zs[c][:, :SB_T] for c in range(n_sub)]
    zd = [jnp.where(col + SB_T < row + off if c == 0 else earlier, zs[c][:, SB_T:], NEG_INF)
          for c in range(n_sub)]
    zl[0] = jnp.where(col < row + off, zl[0], NEG_INF)
    zs = [jnp.concatenate([zl[c], zd[c]], axis=1) for c in range(n_sub)]
    rs = [_dot(_softplus2(zs[c]).astype(bf16), suffix_pair) for c in range(n_sub)]

    def still_live(car, next_tile):
        m = jnp.min(jnp.min(car, axis=1, keepdims=True), axis=0, keepdims=True)
        return ((next_tile >= 0) & (m[0, 0] <= SB_DONE_LOG2)).astype(jnp.int32)

    accs, cars, lives = [], [], []
    for c in range(n_sub):
        a = jnp.exp2(zs[c] - rs[c])
        accs.append(_dot(a.astype(bf16), v_ref[pair_keys[c], :]))
        total = rs[c][:, 0:1]
        cars.append(jnp.broadcast_to(total, (2 * SB_T, SB_T)))
        lives.append(still_live(total, step * n_sub + c - 2))
    for c in range(n_sub):
        q2_scr[c] = q2[c]
        acc_scr[c] = accs[c]
        car_scr[c] = cars[c]

    def one_more_tile(c, d):
        kb = step * n_sub + c - d
        keys = pl.ds(pl.multiple_of(kb * SB_T, SB_T), SB_T)
        z = _dot_nt(q2_scr[c], kn_scr[keys, :])
        full = _dot(_softplus2(z).astype(bf16), suffix2)
        car = car_scr[c]
        a = jnp.exp2(z - (full[:, :SB_T] + car))
        acc_scr[c] += _dot(a.astype(bf16), v_ref[keys, :])
        car = car + full[:, SB_T:]
        car_scr[c] = car
        return still_live(car, kb - 1)

    def more(state):
        live = state[1]
        for flag in state[2:]:
            live = live | flag
        return live > 0

    def walk(state):
        d = state[0]
        flags = [lax.cond(state[1 + c] > 0, functools.partial(one_more_tile, c, d), lambda: jnp.int32(0))
                 for c in range(n_sub)]
        return (d + 1, *flags)

    lax.while_loop(more, walk, (jnp.int32(2), *lives))

    for c in range(n_sub):
        o_ref[c * SB_T:(c + 1) * SB_T, :] = jnp.where(
            lo_mask, acc_scr[c, :SB_T, :], acc_scr[c, SB_T:, :]).astype(o_ref.dtype)


def _sb_attn(proj, q_gain, k_gain, batch, seq):
    t = proj.shape[0]
    hpg = D_MODEL // LANES
    n_pairs = SB_HEADS // 2
    n_qt = seq // SB_QROWS
    n_sub = SB_QROWS // SB_T
    return pl.pallas_call(
        _sb_kernel,
        grid=(batch, n_pairs, n_qt),
        in_specs=[
            pl.BlockSpec((SB_QROWS, LANES), lambda b, p, i: (b * n_qt + i, P_SQ * hpg + p)),
            pl.BlockSpec((seq, LANES), lambda b, p, i: (b, P_SK * hpg + p)),
            pl.BlockSpec((seq, LANES), lambda b, p, i: (b, P_SV * hpg + p)),
            pl.BlockSpec((None, 1, LANES), lambda b, p, i: (p, 0, 0)),
            pl.BlockSpec((None, 1, LANES), lambda b, p, i: (p, 0, 0)),
        ],
        out_specs=pl.BlockSpec((SB_QROWS, LANES), lambda b, p, i: (b * n_qt + i, p)),
        out_shape=jax.ShapeDtypeStruct((t, SB_HEADS * SB_DH), bf16),
        scratch_shapes=[
            pltpu.VMEM((seq, LANES), bf16),
            pltpu.VMEM((n_sub, 2 * SB_T, LANES), bf16),
            pltpu.VMEM((n_sub, 2 * SB_T, LANES), f32),
            pltpu.VMEM((n_sub, 2 * SB_T, LANES), f32),
        ],
        compiler_params=pltpu.CompilerParams(
            dimension_semantics=("arbitrary", "arbitrary", "arbitrary"), vmem_limit_bytes=VMEM_LIMIT),
        name="sb_attn",
    )(proj, proj, proj, q_gain.reshape(n_pairs, 1, LANES), k_gain.reshape(n_pairs, 1, LANES))


def _merge_kernel(x_ref, ohg_ref, osb_ref, ga_ref, gb_ref, whg_ref, wsb_ref, wo_ref, o_ref):
    half = x_ref.shape[0] // 2
    parts = [slice(0, half), slice(half, 2 * half)]
    y_hg = [_dot(ohg_ref[r, :], whg_ref[...]) for r in parts]
    y_sb = [_dot(osb_ref[r, :], wsb_ref[...]) for r in parts]
    for r, yh, ys in zip(parts, y_hg, y_sb):
        mixed = (_sigmoid_tanh(ga_ref[r, :].astype(f32)) * yh
                 + _sigmoid_tanh(gb_ref[r, :].astype(f32)) * ys)
        o_ref[r, :] = x_ref[r, :] + _dot(mixed.astype(bf16), wo_ref[...])


def _merge(x2, o_hg, o_sb, proj, w_hg, w_sb, w_o, tm=1024):
    t = x2.shape[0]
    rows = lambda i: (i, 0)
    resident = dict(pipeline_mode=pl.Buffered(1))
    return pl.pallas_call(
        _merge_kernel,
        grid=(t // tm,),
        in_specs=[
            pl.BlockSpec((tm, D_MODEL), rows),
            pl.BlockSpec((tm, D_MODEL), rows),
            pl.BlockSpec((tm, D_MODEL), rows),
            pl.BlockSpec((tm, D_MODEL), lambda i: (i, P_GA)),
            pl.BlockSpec((tm, D_MODEL), lambda i: (i, P_GB)),
            pl.BlockSpec((D_MODEL, D_MODEL), lambda i: (0, 0), **resident),
            pl.BlockSpec((D_MODEL, D_MODEL), lambda i: (0, 0), **resident),
            pl.BlockSpec((D_MODEL, D_MODEL), lambda i: (0, 0), **resident),
        ],
        out_specs=pl.BlockSpec((tm, D_MODEL), rows),
        out_shape=jax.ShapeDtypeStruct((t, D_MODEL), f32),
        compiler_params=pltpu.CompilerParams(
            dimension_semantics=("arbitrary",), vmem_limit_bytes=VMEM_LIMIT),
        name="merge",
    )(x2, o_hg, o_sb, proj, proj, w_hg, w_sb, w_o)


def _ffn_kernel(x_ref, gain_ref, wi_ref, wd_ref, o_ref, h_scr, act_scr):
    def swiglu(h, rows, j):
        cols = slice(j * FF_CHUNK, (j + 1) * FF_CHUNK)
        up_cols = slice(FF_HIDDEN + j * FF_CHUNK, FF_HIDDEN + (j + 1) * FF_CHUNK)
        gate = _dot(h, wi_ref[:, cols])
        up = _dot(h, wi_ref[:, up_cols])
        act_scr[rows, cols] = (gate * _sigmoid(gate) * up).astype(bf16)

    n_blk = 4
    blk = x_ref.shape[0] // n_blk
    for b in range(n_blk):
        rows = slice(b * blk, (b + 1) * blk)
        h = _rms_norm_bf16(x_ref[rows, :], gain_ref[...])
        h_scr[rows, :] = h
        swiglu(h, rows, 0)
    for j in range(1, FF_HIDDEN // FF_CHUNK):
        swiglu(h_scr[...], slice(None), j)
    o_ref[...] = x_ref[...] + _dot(act_scr[...], wd_ref[...])


def _ffn(x1, gain, w_in_bf, w_out_bf, tm=1024):
    t = x1.shape[0]
    resident = dict(pipeline_mode=pl.Buffered(1))
    return pl.pallas_call(
        _ffn_kernel,
        grid=(t // tm,),
        in_specs=[
            pl.BlockSpec((tm, D_MODEL), lambda i: (i, 0)),
            pl.BlockSpec((1, D_MODEL), lambda i: (0, 0)),
            pl.BlockSpec((D_MODEL, 2 * FF_HIDDEN), lambda i: (0, 0), **resident),
            pl.BlockSpec((FF_HIDDEN, D_MODEL), lambda i: (0, 0), **resident),
        ],
        out_specs=pl.BlockSpec((tm, D_MODEL), lambda i: (i, 0)),
        out_shape=jax.ShapeDtypeStruct((t, D_MODEL), f32),
        scratch_shapes=[pltpu.VMEM((tm, D_MODEL), bf16), pltpu.VMEM((tm, FF_HIDDEN), bf16)],
        compiler_params=pltpu.CompilerParams(
            dimension_semantics=("arbitrary",), vmem_limit_bytes=VMEM_LIMIT),
        name="ffn",
    )(x1, gain, w_in_bf, w_out_bf)


def kernel(x, norm1_gain, w_in, lb_logits, hg_out_norm, sb_q_norm, sb_k_norm,
           w_hg_out, w_sb_out, w_o, norm2_gain, w_ffn_in, w_ffn_out):
    batch, seq, d = x.shape
    depth = norm1_gain.shape[0]
    x2 = x.reshape(batch * seq, d)
    for layer in range(depth):
        gain1 = norm1_gain[layer][None, :]
        proj, h = _in_proj(x2, gain1, w_in[layer])
        o_hg = _hgrn2(proj, h, w_in[layer], lb_logits, hg_out_norm[layer], layer, batch, seq)
        o_sb = _sb_attn(proj, sb_q_norm[layer], sb_k_norm[layer], batch, seq)
        x1 = _merge(x2, o_hg, o_sb, proj, w_hg_out[layer].astype(bf16),
                    w_sb_out[layer].astype(bf16), w_o[layer].astype(bf16))
        x2 = _ffn(x1, norm2_gain[layer][None, :], w_ffn_in[layer].astype(bf16),
                  w_ffn_out[layer].astype(bf16))
    return x2.reshape(batch, seq, d)
```

```python
import functools
import math

import jax
import jax.numpy as jnp
from jax import lax
from jax.experimental import pallas as pl
from jax.experimental.pallas import tpu as pltpu

D_MODEL = 1024
HG_HEADS = 8
HG_DK = 128
HG_DV = 128
SB_HEADS = 16
SB_DH = 64
FF_HIDDEN = 2816
FF_CHUNK = 256
EPS = 1e-6
G_HQ, G_HF, G_HI, G_HG, G_SQ, G_SK, G_SV, G_GA, G_GB = range(9)
P_SOURCE = (G_HQ, G_HI, G_HG, G_SQ, G_SK, G_SV, G_GA, G_GB)
P_HQ, P_HI, P_HG, P_SQ, P_SK, P_SV, P_GA, P_GB = range(8)

LANES = 128
HG_CHUNK = 128
HG_PAR = 4
HG_ROWS = 1024
HG_EXP_CLAMP = 80.0
SB_T = 128
SB_QROWS = 2048
SB_DONE_LOG2 = 128.0
VMEM_LIMIT = 56 * 1024 * 1024

f32 = jnp.float32
bf16 = jnp.bfloat16
NEG_INF = float("-inf")


def _sigmoid(x):
    return 1.0 / (1.0 + jnp.exp(-x))


def _sigmoid_tanh(x):
    return 0.5 * jnp.tanh(0.5 * x) + 0.5


def _dot(a, b):
    return jnp.dot(a, b, preferred_element_type=f32)


def _dot_nt(a, b):
    return lax.dot_general(a, b, (((1,), (1,)), ((), ())), preferred_element_type=f32)


def _dot_tn(a, b):
    return lax.dot_general(a, b, (((0,), (0,)), ((), ())), preferred_element_type=f32)


def _split2(x):
    hi = x.astype(bf16)
    lo = (x - hi.astype(f32)).astype(bf16)
    return hi, lo


def _rms_norm_bf16(x, gain):
    ms = jnp.mean(x * x, axis=-1, keepdims=True)
    return (x * lax.rsqrt(ms + EPS) * gain).astype(bf16)


def _inproj_kernel(x_lo_ref, x_hi_ref, gain_ref, w_ref, o_ref, h_ref):
    j = pl.program_id(1)

    @pl.when(j == 0)
    def _():
        w = w_ref[...].astype(bf16)
        half = x_lo_ref.shape[0]
        blk = half // 2
        for b, x_ref in enumerate((x_lo_ref, x_lo_ref, x_hi_ref, x_hi_ref)):
            src = slice((b % 2) * blk, (b % 2 + 1) * blk)
            rows = slice(b * blk, (b + 1) * blk)
            h = _rms_norm_bf16(x_ref[src, :], gain_ref[...])
            h_ref[rows, :] = h
            o_ref[rows, :] = _dot(h, w).astype(o_ref.dtype)

    is_gate = j == P_HG

    @pl.when(is_gate)
    def _():
        o_ref[...] = _sigmoid_tanh(_dot(h_ref[...], w_ref[...].astype(bf16))).astype(o_ref.dtype)

    @pl.when(jnp.logical_not(is_gate) & (j > 0))
    def _():
        o_ref[...] = _dot(h_ref[...], w_ref[...].astype(bf16)).astype(o_ref.dtype)


def _in_proj(x2, gain, w_in, tm=2048):
    t = x2.shape[0]
    n_groups = len(P_SOURCE)
    n_tiles = t // tm

    def x_half(which, from_group):
        def index(i, j):
            tile = jnp.where(j >= from_group, jnp.minimum(i + 1, n_tiles - 1), i)
            return (2 * tile + which, 0)
        return pl.BlockSpec((tm // 2, D_MODEL), index)

    return pl.pallas_call(
        _inproj_kernel,
        grid=(n_tiles, n_groups),
        in_specs=[
            x_half(0, 2),
            x_half(1, 4),
            pl.BlockSpec((1, D_MODEL), lambda i, j: (0, 0)),
            pl.BlockSpec((D_MODEL, D_MODEL), lambda i, j: (0, jnp.where(j >= G_HF, j + 1, j))),
        ],
        out_specs=[
            pl.BlockSpec((tm, D_MODEL), lambda i, j: (i, j)),
            pl.BlockSpec((tm, D_MODEL), lambda i, j: (i, 0)),
        ],
        out_shape=[
            jax.ShapeDtypeStruct((t, n_groups * D_MODEL), bf16),
            jax.ShapeDtypeStruct((t, D_MODEL), bf16),
        ],
        compiler_params=pltpu.CompilerParams(
            dimension_semantics=("arbitrary", "arbitrary"), vmem_limit_bytes=VMEM_LIMIT),
        name="in_proj",
    )(x2, x2, gain, w_in)


def _hgrn_kernel(q_ref, h_ref, i_ref, og_ref, w_ref, lbl_ref, gain_ref, o_ref, st_scr, w_scr, fg_scr,
                 *, layer):
    c_len = HG_CHUNK
    blk = HG_PAR * c_len
    n_blk = q_ref.shape[0] // blk

    @pl.when((pl.program_id(0) == 0) & (pl.program_id(1) == 0))
    def _():
        w_scr[...] = w_ref[...].astype(bf16)

    @pl.when(pl.program_id(1) == 0)
    def _():
        st_scr[...] = jnp.zeros_like(st_scr)

    l = lbl_ref[...]
    e = jnp.exp(l - jnp.max(l, axis=0, keepdims=True))
    lb = jnp.sum(e[0:layer + 1, :], axis=0, keepdims=True) / jnp.sum(e, axis=0, keepdims=True)
    f_mid = 0.5 * (1.0 + lb)
    f_amp = 0.5 * (1.0 - lb)
    gain = gain_ref[...]

    def forget_block(it, slot):
        rows = pl.ds(pl.multiple_of(it * blk, blk), blk)
        f = f_mid + f_amp * jnp.tanh(0.5 * _dot(h_ref[rows, :], w_scr[...]))
        fg_scr[slot, 0], fg_scr[slot, 1] = _split2(jnp.log(f))
        fg_scr[slot, 2] = (1.0 - f).astype(bf16)

    row = lax.broadcasted_iota(jnp.int32, (c_len, c_len), 0)
    col = lax.broadcasted_iota(jnp.int32, (c_len, c_len), 1)
    causal = row >= col
    tri = jnp.where(causal, 1.0, 0.0).astype(bf16)
    tri2 = jnp.concatenate([tri, tri], axis=1)
    heads = range(HG_HEADS)
    sl = [slice(h * HG_DK, (h + 1) * HG_DK) for h in heads]

    def chunk_operands(it, p, slot):
        rows = pl.ds(pl.multiple_of(it * blk + p * c_len, c_len), c_len)
        in_blk = slice(p * c_len, (p + 1) * c_len)
        cum = _dot(tri2, jnp.concatenate([fg_scr[slot, 0, in_blk, :], fg_scr[slot, 1, in_blk, :]], axis=0))
        k = fg_scr[slot, 2, in_blk, :].astype(f32)
        last = cum[c_len - 1:c_len, :]
        mid = 0.5 * last
        q = q_ref[rows, :].astype(f32)
        return dict(
            rows=rows,
            qt=(q * jnp.exp(jnp.minimum(cum - mid, HG_EXP_CLAMP))).astype(bf16),
            kt=(k * jnp.exp(jnp.minimum(mid - cum, HG_EXP_CLAMP))).astype(bf16),
            qd=(q * jnp.exp(cum)).astype(bf16),
            kd=(k * jnp.exp(last - cum)).astype(bf16),
            e_last=jnp.exp(last),
            v=i_ref[rows, :],
            out_scale=gain * og_ref[rows, :].astype(f32),
        )

    def row_block(it, has_next):
        slot = it % 2
        cs = [chunk_operands(it, p, slot) for p in range(HG_PAR)]
        if has_next:
            forget_block(it + 1, 1 - slot)
        scores =[[jnp.where(causal, _dot_nt(c["qt"][:, sl[h]], c["kt"][:, sl[h]]), 0.0).astype(bf16)
                   for h in heads] for c in cs]
        upd = [[_dot_tn(c["v"][:, sl[h]], c["kd"][:, sl[h]]) for h in heads] for c in cs]
        states = [st_scr[h] for h in heads]
        inter = []
        for p, c in enumerate(cs):
            inter.append([_dot_nt(c["qd"][:, sl[h]], states[h].astype(bf16)) for h in heads])
            states = [states[h] * c["e_last"][:, sl[h]] + upd[p][h] for h in heads]
        for h in heads:
            st_scr[h] = states[h]
        for p, c in enumerate(cs):
            outs = []
            for h in heads:
                o = _dot(scores[p][h], c["v"][:, sl[h]]) + inter[p][h]
                ms = jnp.mean(o * o, axis=-1, keepdims=True)
                outs.append((o * lax.rsqrt(ms + EPS) * c["out_scale"][:, sl[h]]).astype(o_ref.dtype))
            o_ref[c["rows"], :] = jnp.concatenate(outs, axis=1)

    def body(it, carry):
        row_block(it, True)
        return carry

    forget_block(0, 0)
    lax.fori_loop(0, n_blk - 1, body, 0)
    row_block(n_blk - 1, False)


def _hgrn2(proj, h, w_in, lb_logits, hg_gain, layer, batch, seq):
    t = proj.shape[0]
    n_rt = seq // HG_ROWS

    def grp(group):
        return pl.BlockSpec((HG_ROWS, D_MODEL), lambda b, r: (b * n_rt + r, group))

    return pl.pallas_call(
        functools.partial(_hgrn_kernel, layer=layer),
        grid=(batch, n_rt),
        in_specs=[
            grp(P_HQ), grp(0), grp(P_HI), grp(P_HG),
            pl.BlockSpec((D_MODEL, D_MODEL), lambda b, r: (0, G_HF), pipeline_mode=pl.Buffered(1)),
            pl.BlockSpec(lb_logits.shape, lambda b, r: (0, 0)),
            pl.BlockSpec((1, D_MODEL), lambda b, r: (0, 0)),
        ],
        out_specs=grp(0),
        out_shape=jax.ShapeDtypeStruct((t, HG_HEADS * HG_DV), bf16),
        scratch_shapes=[
            pltpu.VMEM((HG_HEADS, HG_DV, HG_DK), f32),
            pltpu.VMEM((D_MODEL, D_MODEL), bf16),
            pltpu.VMEM((2, 3, HG_PAR * HG_CHUNK, D_MODEL), bf16),
        ],
        compiler_params=pltpu.CompilerParams(
            dimension_semantics=("arbitrary", "arbitrary"), vmem_limit_bytes=VMEM_LIMIT),
        name="hgrn2",
    )(proj, h, proj, proj, w_in, lb_logits, hg_gain.reshape(1, HG_HEADS * HG_DV))


def _pair_rms(blocks, gain, group_mean):
    ms = [_dot((x * x).astype(bf16), group_mean) for x in blocks]
    return [x * lax.rsqrt(m + EPS) * gain for x, m in zip(blocks, ms)]


def _softplus2(z):
    return jnp.maximum(z, 0.0) + jnp.log2(1.0 + jnp.exp2(-jnp.abs(z)))


def _sb_kernel(q_ref, k_ref, v_ref, qg_ref, kg_ref, o_ref, kn_scr, q2_scr, acc_scr, car_scr):
    step = pl.program_id(2)
    seq = k_ref.shape[0]
    n_sub = SB_QROWS // SB_T
    lane = lax.broadcasted_iota(jnp.int32, (1, LANES), 1)
    lo_mask = lane < SB_DH

    gr = lax.broadcasted_iota(jnp.int32, (LANES, LANES), 0) // SB_DH
    gc = lax.broadcasted_iota(jnp.int32, (LANES, LANES), 1) // SB_DH
    group_mean = jnp.where(gr == gc, 1.0 / SB_DH, 0.0).astype(bf16)

    @pl.when(step == 0)
    def _():
        prep = 512

        blocks = [k_ref[r * prep:(r + 1) * prep, :].astype(f32) for r in range(seq // prep)]
        for r, kn in enumerate(_pair_rms(blocks, kg_ref[...], group_mean)):
            kn_scr[r * prep:(r + 1) * prep, :] = kn.astype(bf16)

    q_scale = math.log2(math.e) / math.sqrt(SB_DH)
    qn = _pair_rms([q_ref[c * SB_T:(c + 1) * SB_T, :].astype(f32) for c in range(n_sub)],
                   qg_ref[...] * q_scale, group_mean)
    q2 = [jnp.concatenate([jnp.where(lo_mask, q, 0.0).astype(bf16),
                           jnp.where(lo_mask, 0.0, q).astype(bf16)], axis=0) for q in qn]

    row = lax.broadcasted_iota(jnp.int32, (2 * SB_T, SB_T), 0) % SB_T
    col = lax.broadcasted_iota(jnp.int32, (2 * SB_T, SB_T), 1)
    earlier = col < row
    urow = lax.broadcasted_iota(jnp.int32, (2 * SB_T, 2 * SB_T), 0)
    ucol = lax.broadcasted_iota(jnp.int32, (2 * SB_T, 2 * SB_T), 1)
    suffix2 = jnp.where((urow >= ucol) | (ucol >= SB_T), 1.0, 0.0).astype(bf16)[:SB_T]
    suffix_pair = jnp.where((urow >= ucol) & ((urow < SB_T) == (ucol < SB_T)) | (urow >= SB_T) & (ucol < SB_T),
                            1.0, 0.0).astype(bf16)

    first = step == 0
    k_start = [jnp.where(first, 0, step * n_sub - 1) * SB_T] + [
        (step * n_sub + c - 1) * SB_T for c in range(1, n_sub)]
    pair_keys = [pl.ds(pl.multiple_of(k0, SB_T), 2 * SB_T) for k0 in k_start]

    zs = [_dot_nt(q2[c], kn_scr[pair_keys[c], :]) for c in range(n_sub)]
    off = jnp.where(first, 0, SB_T)
    zl = [zs[c][:, :SB_T] for c in range(n_sub)]
    zd = [jnp.where(col + SB_T < row + off if c == 0 else earlier, zs[c][:, SB_T:], NEG_INF)
          for c in range(n_sub)]
    zl[0] = jnp.where(col < row + off, zl[0], NEG_INF)
    zs = [jnp.concatenate([zl[c], zd[c]], axis=1) for c in range(n_sub)]
    rs = [_dot(_softplus2(zs[c]).astype(bf16), suffix_pair) for c in range(n_sub)]

    def still_live(car, next_tile):
        m = jnp.min(jnp.min(car, axis=1, keepdims=True), axis=0, keepdims=True)
        return ((next_tile >= 0) & (m[0, 0] <= SB_DONE_LOG2)).astype(jnp.int32)

    accs, cars, lives = [], [], []
    for c in range(n_sub):
        a = jnp.exp2(zs[c] - rs[c])
        accs.append(_dot(a.astype(bf16), v_ref[pair_keys[c], :]))
        total = rs[c][:, 0:1]
        cars.append(jnp.broadcast_to(total, (2 * SB_T, SB_T)))
        lives.append(still_live(total, step * n_sub + c - 2))
    for c in range(n_sub):
        q2_scr[c] = q2[c]
        acc_scr[c] = accs[c]
        car_scr[c] = cars[c]

    def one_more_tile(c, d):
        kb = step * n_sub + c - d
        keys = pl.ds(pl.multiple_of(kb * SB_T, SB_T), SB_T)
        z = _dot_nt(q2_scr[c], kn_scr[keys, :])
        full = _dot(_softplus2(z).astype(bf16), suffix2)
        car = car_scr[c]
        a = jnp.exp2(z - (full[:, :SB_T] + car))
        acc_scr[c] += _dot(a.astype(bf16), v_ref[keys, :])
        car = car + full[:, SB_T:]
        car_scr[c] = car
        return still_live(car, kb - 1)

    def more(state):
        live = state[1]
        for flag in state[2:]:
            live = live | flag
        return live > 0

    def walk(state):
        d = state[0]
        flags = [lax.cond(state[1 + c] > 0, functools.partial(one_more_tile, c, d), lambda: jnp.int32(0))
                 for c in range(n_sub)]
        return (d + 1, *flags)

    lax.while_loop(more, walk, (jnp.int32(2), *lives))

    for c in range(n_sub):
        o_ref[c * SB_T:(c + 1) * SB_T, :] = jnp.where(
            lo_mask, acc_scr[c, :SB_T, :], acc_scr[c, SB_T:, :]).astype(o_ref.dtype)


def _sb_attn(proj, q_gain, k_gain, batch, seq):
    t = proj.shape[0]
    hpg = D_MODEL // LANES
    n_pairs = SB_HEADS // 2
    n_qt = seq // SB_QROWS
    n_sub = SB_QROWS // SB_T
    return pl.pallas_call(
        _sb_kernel,
        grid=(batch, n_pairs, n_qt),
        in_specs=[
            pl.BlockSpec((SB_QROWS, LANES), lambda b, p, i: (b * n_qt + i, P_SQ * hpg + p)),
            pl.BlockSpec((seq, LANES), lambda b, p, i: (b, P_SK * hpg + p)),
            pl.BlockSpec((seq, LANES), lambda b, p, i: (b, P_SV * hpg + p)),
            pl.BlockSpec((None, 1, LANES), lambda b, p, i: (p, 0, 0)),
            pl.BlockSpec((None, 1, LANES), lambda b, p, i: (p, 0, 0)),
        ],
        out_specs=pl.BlockSpec((SB_QROWS, LANES), lambda b, p, i: (b * n_qt + i, p)),
        out_shape=jax.ShapeDtypeStruct((t, SB_HEADS * SB_DH), bf16),
        scratch_shapes=[
            pltpu.VMEM((seq, LANES), bf16),
            pltpu.VMEM((n_sub, 2 * SB_T, LANES), bf16),
            pltpu.VMEM((n_sub, 2 * SB_T, LANES), f32),
            pltpu.VMEM((n_sub, 2 * SB_T, LANES), f32),
        ],
        compiler_params=pltpu.CompilerParams(
            dimension_semantics=("arbitrary", "arbitrary", "arbitrary"), vmem_limit_bytes=VMEM_LIMIT),
        name="sb_attn",
    )(proj, proj, proj, q_gain.reshape(n_pairs, 1, LANES), k_gain.reshape(n_pairs, 1, LANES))


def _merge_kernel(x_ref, ohg_ref, osb_ref, ga_ref, gb_ref, whg_ref, wsb_ref, wo_ref, o_ref):
    half = x_ref.shape[0] // 2
    parts = [slice(0, half), slice(half, 2 * half)]
    y_hg = [_dot(ohg_ref[r, :], whg_ref[...]) for r in parts]
    y_sb = [_dot(osb_ref[r, :], wsb_ref[...]) for r in parts]
    for r, yh, ys in zip(parts, y_hg, y_sb):
        mixed = (_sigmoid_tanh(ga_ref[r, :].astype(f32)) * yh
                 + _sigmoid_tanh(gb_ref[r, :].astype(f32)) * ys)
        o_ref[r, :] = x_ref[r, :] + _dot(mixed.astype(bf16), wo_ref[...])


def _merge(x2, o_hg, o_sb, proj, w_hg, w_sb, w_o, tm=1024):
    t = x2.shape[0]
    rows = lambda i: (i, 0)
    resident = dict(pipeline_mode=pl.Buffered(1))
    return pl.pallas_call(
        _merge_kernel,
        grid=(t // tm,),
        in_specs=[
            pl.BlockSpec((tm, D_MODEL), rows),
            pl.BlockSpec((tm, D_MODEL), rows),
            pl.BlockSpec((tm, D_MODEL), rows),
            pl.BlockSpec((tm, D_MODEL), lambda i: (i, P_GA)),
            pl.BlockSpec((tm, D_MODEL), lambda i: (i, P_GB)),
            pl.BlockSpec((D_MODEL, D_MODEL), lambda i: (0, 0), **resident),
            pl.BlockSpec((D_MODEL, D_MODEL), lambda i: (0, 0), **resident),
            pl.BlockSpec((D_MODEL, D_MODEL), lambda i: (0, 0), **resident),
        ],
        out_specs=pl.BlockSpec((tm, D_MODEL), rows),
        out_shape=jax.ShapeDtypeStruct((t, D_MODEL), f32),
        compiler_params=pltpu.CompilerParams(
            dimension_semantics=("arbitrary",), vmem_limit_bytes=VMEM_LIMIT,
            allow_input_fusion=[False] * 5 + [True] * 3),
        name="merge",
    )(x2, o_hg, o_sb, proj, proj, w_hg, w_sb, w_o)


def _ffn_kernel(x_ref, gain_ref, wi_ref, wd_ref, o_ref, h_scr, act_scr):
    def swiglu(h, rows, j):
        cols = slice(j * FF_CHUNK, (j + 1) * FF_CHUNK)
        up_cols = slice(FF_HIDDEN + j * FF_CHUNK, FF_HIDDEN + (j + 1) * FF_CHUNK)
        gate = _dot(h, wi_ref[:, cols])
        up = _dot(h, wi_ref[:, up_cols])
        act_scr[rows, cols] = (gate * _sigmoid(gate) * up).astype(bf16)

    n_blk = 4
    blk = x_ref.shape[0] // n_blk
    for b in range(n_blk):
        rows = slice(b * blk, (b + 1) * blk)
        h = _rms_norm_bf16(x_ref[rows, :], gain_ref[...])
        h_scr[rows, :] = h
        swiglu(h, rows, 0)
    for j in range(1, FF_HIDDEN // FF_CHUNK):
        swiglu(h_scr[...], slice(None), j)
    o_ref[...] = x_ref[...] + _dot(act_scr[...], wd_ref[...])


def _ffn(x1, gain, w_in_bf, w_out_bf, tm=1024):
    t = x1.shape[0]
    resident = dict(pipeline_mode=pl.Buffered(1))
    return pl.pallas_call(
        _ffn_kernel,
        grid=(t // tm,),
        in_specs=[
            pl.BlockSpec((tm, D_MODEL), lambda i: (i, 0)),
            pl.BlockSpec((1, D_MODEL), lambda i: (0, 0)),
            pl.BlockSpec((D_MODEL, 2 * FF_HIDDEN), lambda i: (0, 0), **resident),
            pl.BlockSpec((FF_HIDDEN, D_MODEL), lambda i: (0, 0), **resident),
        ],
        out_specs=pl.BlockSpec((tm, D_MODEL), lambda i: (i, 0)),
        out_shape=jax.ShapeDtypeStruct((t, D_MODEL), f32),
        scratch_shapes=[pltpu.VMEM((tm, D_MODEL), bf16), pltpu.VMEM((tm, FF_HIDDEN), bf16)],
        compiler_params=pltpu.CompilerParams(
            dimension_semantics=("arbitrary",), vmem_limit_bytes=VMEM_LIMIT,
            allow_input_fusion=[False, False, True, True]),
        name="ffn",
    )(x1, gain, w_in_bf, w_out_bf)


def kernel(x, norm1_gain, w_in, lb_logits, hg_out_norm, sb_q_norm, sb_k_norm,
           w_hg_out, w_sb_out, w_o, norm2_gain, w_ffn_in, w_ffn_out):
    batch, seq, d = x.shape
    depth = norm1_gain.shape[0]
    x2 = x.reshape(batch * seq, d)
    for layer in range(depth):
        gain1 = norm1_gain[layer][None, :]
        proj, h = _in_proj(x2, gain1, w_in[layer])
        o_hg = _hgrn2(proj, h, w_in[layer], lb_logits, hg_out_norm[layer], layer, batch, seq)
        o_sb = _sb_attn(proj, sb_q_norm[layer], sb_k_norm[layer], batch, seq)
        x1 = _merge(x2, o_hg, o_sb, proj, w_hg_out[layer].astype(bf16),
                    w_sb_out[layer].astype(bf16), w_o[layer].astype(bf16))
        x2 = _ffn(x1, norm2_gain[layer][None, :], w_ffn_in[layer].astype(bf16),
                  w_ffn_out[layer].astype(bf16))
    return x2.reshape(batch, seq, d)
```
